```python
import jax, jax.numpy as jnp
from jax import lax
import numpy as np

D_MODEL = 1024
BATCH = 4
SEQ = 4096
DEPTH = 4
DEC_BATCH = 32
DEC_SEQ = 4
PAST_LEN = 8192
PAGE_SIZE = 128

D_MIX = D_MODEL
D_RET = D_MIX // 2
H_R = 4
DK_R = D_RET // H_R
D_FOX = D_MIX - D_RET
DH_F = 64
H_F = D_FOX // DH_F
D_IN = 4 * D_RET + 4 * D_FOX + H_F
RET_CHUNK = 128
Q_BLOCK = 128
ROPE_BASE = 10000.0
EPS = 1e-6

kernel_name = "hymba_retnet_fox_step"


def rms_norm(x, g):
    xf = x.astype(jnp.float32)
    y = xf * lax.rsqrt(jnp.mean(xf * xf, axis=-1, keepdims=True) + EPS)
    return (y * g.astype(jnp.float32)).astype(x.dtype)


def rotary(x, pos):
    d = x.shape[-1]
    inv = 1.0 / (ROPE_BASE ** (jnp.arange(0, d, 2, dtype=jnp.float32) / d))
    ang = pos[:, None] * inv[None, :]
    cos = jnp.cos(ang)[None, :, None, :]
    sin = jnp.sin(ang)[None, :, None, :]
    xf = x.astype(jnp.float32)
    x1, x2 = xf[..., : d // 2], xf[..., d // 2:]
    return jnp.concatenate([x1 * cos - x2 * sin, x1 * sin + x2 * cos], axis=-1).astype(x.dtype)


def retention_log_decay():
    return jnp.log(1.0 - 2.0 ** (-5.0 - jnp.arange(H_R, dtype=jnp.float32)))


def split_columns(z):
    sizes = (D_RET,) * 4 + (D_FOX,) * 4 + (H_F,)
    offs = np.cumsum(sizes)[:-1].tolist()
    return jnp.split(z, offs, axis=-1)


def project(x, norm_g, w_in_l, fq_g, fk_g, bf_l, pos):
    b, t, _ = x.shape
    h = rms_norm(x, norm_g)
    z = h @ w_in_l
    rq, rk, rv, rg, fq, fk, fv, fg, ff = split_columns(z)
    rq = rotary(rq.reshape(b, t, H_R, DK_R), pos)
    rk = rotary(rk.reshape(b, t, H_R, DK_R), pos) * (DK_R ** -0.5)
    rv = rv.reshape(b, t, H_R, DK_R)
    fq = rms_norm(fq.reshape(b, t, H_F, DH_F), fq_g)
    fk = rms_norm(fk.reshape(b, t, H_F, DH_F), fk_g)
    fv = fv.reshape(b, t, H_F, DH_F)
    logf = jax.nn.log_sigmoid(ff.astype(jnp.float32) + bf_l.astype(jnp.float32))
    return rq, rk, rv, rg, fq, fk, fv, fg, logf


def retention_chunk(q, k, v, s, log_gamma):
    c = q.shape[1]
    idx = jnp.arange(c, dtype=jnp.float32)
    diff = idx[:, None] - idx[None, :]
    decay = jnp.where(diff >= 0, jnp.exp(jnp.maximum(diff, 0.0)[None] * log_gamma[:, None, None]), 0.0)
    scores = jnp.einsum('bihd,bjhd->bhij', q, k) * decay[None]
    o_inner = jnp.einsum('bhij,bjhe->bihe', scores, v)
    q_decay = jnp.exp((idx + 1.0)[:, None] * log_gamma[None, :])
    o_cross = jnp.einsum('bihd,bhde->bihe', q, s) * q_decay[None, :, :, None]
    k_decay = jnp.exp((c - 1.0 - idx)[:, None] * log_gamma[None, :])
    s_new = (jnp.exp(c * log_gamma)[None, :, None, None] * s
             + jnp.einsum('bjhd,bjhe->bhde', k * k_decay[None, :, :, None], v))
    return o_inner + o_cross, s_new


def retention_prompt(q, k, v, log_gamma):
    b, t, h, dk = q.shape
    dv = v.shape[-1]
    nc = t // RET_CHUNK

    def chunks(a):
        return a.astype(jnp.float32).reshape(b, nc, RET_CHUNK, h, a.shape[-1]).swapaxes(0, 1)

    s0 = jnp.zeros((b, h, dk, dv), jnp.float32)

    def step(s, blk):
        o, s2 = retention_chunk(blk[0], blk[1], blk[2], s, log_gamma)
        return s2, o

    s_fin, o = lax.scan(step, s0, (chunks(q), chunks(k), chunks(v)))
    return o.swapaxes(0, 1).reshape(b, t, h, dv), s_fin


def fox_prompt(q, k, v, logf):
    b, t, h, d = q.shape
    nb = t // Q_BLOCK
    c = lax.cumsum(logf, axis=1)
    ck = c.transpose(0, 2, 1)
    qb = q.reshape(b, nb, Q_BLOCK, h, d).swapaxes(0, 1)
    cqb = c.reshape(b, nb, Q_BLOCK, h).swapaxes(0, 1)
    starts = jnp.arange(nb, dtype=jnp.int32) * Q_BLOCK
    kpos = jnp.arange(t, dtype=jnp.int32)
    scale = d ** -0.5

    def blk(args):
        qi, cqi, st = args
        s = jnp.einsum('bqhd,bkhd->bhqk', qi, k).astype(jnp.float32) * scale
        s = s + cqi.transpose(0, 2, 1)[..., None] - ck[:, :, None, :]
        qpos = st + jnp.arange(Q_BLOCK, dtype=jnp.int32)
        s = jnp.where(kpos[None, :] <= qpos[:, None], s, -jnp.inf)
        p = jax.nn.softmax(s, axis=-1)
        return jnp.einsum('bhqk,bkhd->bqhd', p.astype(v.dtype), v)

    out = lax.map(blk, (qb, cqb, starts))
    return out.swapaxes(0, 1).reshape(b, t, h, d)


def fox_sample(q, k, v, logf, k_past, v_past, logf_past):
    t = q.shape[1]
    d = q.shape[-1]
    scale = d ** -0.5
    c_past = lax.cumsum(logf_past.astype(jnp.float32), axis=1)
    c_new = c_past[:, -1:, :] + lax.cumsum(logf, axis=1)
    cq = c_new.transpose(0, 2, 1)[..., None]
    k_past = k_past.astype(k.dtype)
    v_past = v_past.astype(v.dtype)
    s_past = jnp.einsum('bqhd,bkhd->bhqk', q, k_past).astype(jnp.float32) * scale
    s_past = s_past + cq - c_past.transpose(0, 2, 1)[:, :, None, :]
    s_new = jnp.einsum('bqhd,bkhd->bhqk', q, k).astype(jnp.float32) * scale
    s_new = s_new + cq - c_new.transpose(0, 2, 1)[:, :, None, :]
    causal = jnp.arange(t)[None, :] <= jnp.arange(t)[:, None]
    s_new = jnp.where(causal, s_new, -jnp.inf)
    p = jax.nn.softmax(jnp.concatenate([s_past, s_new], axis=-1), axis=-1)
    n_past = k_past.shape[1]
    p_past = p[..., :n_past].astype(v.dtype)
    p_new = p[..., n_past:].astype(v.dtype)
    return (jnp.einsum('bhqk,bkhd->bqhd', p_past, v_past)
            + jnp.einsum('bhqk,bkhd->bqhd', p_new, v))


def merge(x, o_ret, rg, o_fox, fg, ret_g, w_out_l):
    b, t, _ = x.shape
    o_ret = rms_norm(o_ret, ret_g).astype(x.dtype).reshape(b, t, D_RET)
    o_fox = o_fox.astype(x.dtype).reshape(b, t, D_FOX)
    u = jnp.concatenate([o_ret * jax.nn.silu(rg), o_fox * jax.nn.silu(fg)], axis=-1)
    return x + u @ w_out_l


def setup_inputs(seed: int = 0) -> dict:
    key = jax.random.key(seed)
    ks = jax.random.split(key, 16)
    n_pages = PAST_LEN // PAGE_SIZE
    n_used = DEC_BATCH * n_pages
    n_pool = n_used + max(1, n_used // 4)
    x_prompt = jax.random.normal(ks[0], (BATCH, SEQ, D_MODEL), jnp.float32)
    x_sample = jax.random.normal(ks[1], (DEC_BATCH, DEC_SEQ, D_MODEL), jnp.float32)
    fox_b_f = jnp.linspace(1.0, 6.0, H_F, dtype=jnp.float32)[None, :] + 0.1 * jax.random.normal(ks[2], (DEPTH, H_F), jnp.float32)
    cache_fox_k = jax.random.normal(ks[3], (DEPTH, n_pool, PAGE_SIZE, H_F, DH_F), jnp.float32)
    cache_fox_v = jax.random.normal(ks[4], (DEPTH, n_pool, PAGE_SIZE, H_F, DH_F), jnp.float32)
    cache_fox_logf = jax.nn.log_sigmoid(fox_b_f[:, None, None, :] + jax.random.normal(ks[5], (DEPTH, n_pool, PAGE_SIZE, H_F), jnp.float32))
    state_ret = jax.random.normal(ks[6], (DEPTH, DEC_BATCH, H_R, DK_R, DK_R), jnp.float32)
    page_table = jax.random.permutation(ks[7], n_pool)[:n_used].reshape(DEC_BATCH, n_pages).astype(jnp.int32)
    norm_gain = 1.0 + 0.1 * jax.random.normal(ks[8], (DEPTH, D_MODEL), jnp.float32)
    w_in = jax.random.normal(ks[9], (DEPTH, D_MODEL, D_IN), jnp.float32) * (D_MODEL ** -0.5)
    ret_norm_gain = 1.0 + 0.1 * jax.random.normal(ks[10], (DEPTH, H_R, DK_R), jnp.float32)
    fox_q_gain = 1.0 + 0.1 * jax.random.normal(ks[11], (DEPTH, DH_F), jnp.float32)
    fox_k_gain = 1.0 + 0.1 * jax.random.normal(ks[12], (DEPTH, DH_F), jnp.float32)
    w_out = jax.random.normal(ks[13], (DEPTH, D_MIX, D_MODEL), jnp.float32) * (D_MIX ** -0.5)
    return {"x_prompt": x_prompt, "x_sample": x_sample, "cache_fox_k": cache_fox_k,
            "cache_fox_v": cache_fox_v, "cache_fox_logf": cache_fox_logf, "state_ret": state_ret,
            "page_table": page_table, "norm_gain": norm_gain, "w_in": w_in,
            "ret_norm_gain": ret_norm_gain, "fox_q_gain": fox_q_gain, "fox_k_gain": fox_k_gain,
            "fox_b_f": fox_b_f, "w_out": w_out}


def reference(x_prompt, x_sample, cache_fox_k, cache_fox_v, cache_fox_logf, state_ret, page_table,
              norm_gain, w_in, ret_norm_gain, fox_q_gain, fox_k_gain, fox_b_f, w_out):
    log_gamma = retention_log_decay()
    t_p = x_prompt.shape[1]
    db, t_s = x_sample.shape[0], x_sample.shape[1]
    past_len = page_table.shape[1] * PAGE_SIZE
    pos_p = jnp.arange(t_p, dtype=jnp.float32)
    pos_s = past_len + jnp.arange(t_s, dtype=jnp.float32)
    xp, xs = x_prompt, x_sample
    kp_l, vp_l, fp_l, sp_l = [], [], [], []
    ks_l, vs_l, fs_l, ss_l = [], [], [], []
    for l in range(DEPTH):
        rq, rk, rv, rg, fq, fk, fv, fg, lf = project(xp, norm_gain[l], w_in[l], fox_q_gain[l], fox_k_gain[l], fox_b_f[l], pos_p)
        o_r, s_p = retention_prompt(rq, rk, rv, log_gamma)
        o_f = fox_prompt(fq, fk, fv, lf)
        xp = merge(xp, o_r, rg, o_f, fg, ret_norm_gain[l], w_out[l])
        kp_l.append(fk); vp_l.append(fv); fp_l.append(lf); sp_l.append(s_p)
        rq, rk, rv, rg, fq, fk, fv, fg, lf = project(xs, norm_gain[l], w_in[l], fox_q_gain[l], fox_k_gain[l], fox_b_f[l], pos_s)
        o_r, s_s = retention_chunk(rq.astype(jnp.float32), rk.astype(jnp.float32), rv.astype(jnp.float32),
                                   state_ret[l].astype(jnp.float32), log_gamma)
        k_past = cache_fox_k[l, page_table].reshape(db, past_len, H_F, DH_F)
        v_past = cache_fox_v[l, page_table].reshape(db, past_len, H_F, DH_F)
        lf_past = cache_fox_logf[l, page_table].reshape(db, past_len, H_F)
        o_f = fox_sample(fq, fk, fv, lf, k_past, v_past, lf_past)
        xs = merge(xs, o_r, rg, o_f, fg, ret_norm_gain[l], w_out[l])
        ks_l.append(fk); vs_l.append(fv); fs_l.append(lf); ss_l.append(s_s)
    return (xp, xs,
            jnp.stack(kp_l), jnp.stack(vp_l), jnp.stack(fp_l), jnp.stack(sp_l),
            jnp.stack(ks_l), jnp.stack(vs_l), jnp.stack(fs_l), jnp.stack(ss_l))
```

```python
import functools

import jax
import jax.numpy as jnp
from jax import lax
from jax.experimental import pallas as pl
from jax.experimental.pallas import tpu as pltpu

F32 = jnp.float32
BF16 = jnp.bfloat16

H_R = 4
DK_R = 128
H_F = 8
DH_F = 64
D_RET = H_R * DK_R
D_FOX = H_F * DH_F
PAGE = 128
RET_CHUNK = 128
ROPE_BASE = 10000.0
EPS = 1e-6
NEG = -1e30

LANES = 128
VMEM_LIMIT = 56 * 1024 * 1024

NT_DIMS = (((1,), (1,)), ((), ()))
TN_DIMS = (((0,), (0,)), ((), ()))


def _cparams(sem):
    return pltpu.CompilerParams(dimension_semantics=sem, vmem_limit_bytes=VMEM_LIMIT)


def _seg_cumsum(x, seg):
    lane = lax.broadcasted_iota(jnp.int32, x.shape, 1)
    pos = lane & (seg - 1)
    d = 1
    while d < seg:
        shifted = pltpu.roll(x, d, axis=1)
        x = x + jnp.where(pos >= d, shifted, 0.0)
        d *= 2
    return x


def _proj_kernel(x_ref, ng_ref, w_ref, wff_ref, bf_ref, cos_ref, sin_ref, qg_ref, kg_ref, ones_ref,
                 rq_ref, rk_ref, rv_ref, srg_ref, fq_ref, fk_ref, fkb_ref, fv_ref, fvb_ref, sfg_ref,
                 lf_ref, c_ref, carry_ref, *, seg, tiles_per_seq):
    i = pl.program_id(0)

    @pl.when(i % tiles_per_seq == 0)
    def _():
        carry_ref[...] = jnp.zeros_like(carry_ref)

    x = x_ref[...]
    ms = jnp.mean(x * x, axis=-1, keepdims=True)
    h = (x * lax.rsqrt(ms + EPS) * ng_ref[...]).astype(BF16)

    def zcol(g):
        return jnp.dot(h, w_ref[:, g * D_RET:(g + 1) * D_RET], preferred_element_type=F32)

    cos = cos_ref[...]
    sin = sin_ref[...]

    def rope_store(z, out_ref, scale):
        for hh in range(H_R):
            zz = z[:, hh * DK_R:(hh + 1) * DK_R]
            rot = pltpu.roll(zz, DK_R // 2, axis=1)
            r = zz * cos + rot * sin
            if scale is not None:
                r = r * scale
            out_ref[:, hh * DK_R:(hh + 1) * DK_R] = r.astype(out_ref.dtype)

    def silu(z):
        return z / (1.0 + jnp.exp(-z))

    def head_norm(z, gain):
        y = z * z
        yh = y.astype(BF16)
        yl = (y - yh.astype(F32)).astype(BF16)
        ssq = (jnp.dot(yh, ones_ref[...], preferred_element_type=F32)
               + jnp.dot(yl, ones_ref[...], preferred_element_type=F32))
        return z * lax.rsqrt(ssq * (1.0 / DH_F) + EPS) * gain

    rope_store(zcol(0), rq_ref, None)
    rope_store(zcol(1), rk_ref, DK_R ** -0.5)
    rv_ref[...] = zcol(2).astype(BF16)
    srg_ref[...] = silu(zcol(3)).astype(BF16)
    fq_ref[...] = head_norm(zcol(4), qg_ref[...]).astype(BF16)
    fk = head_norm(zcol(5), kg_ref[...])
    fk_ref[...] = fk
    fkb_ref[...] = fk.astype(BF16)
    fv = zcol(6)
    fv_ref[...] = fv
    fvb_ref[...] = fv.astype(BF16)
    sfg_ref[...] = silu(zcol(7)).astype(BF16)

    ff = lax.dot_general(wff_ref[...], h, NT_DIMS, preferred_element_type=F32) + bf_ref[...]
    lf = jnp.minimum(ff, 0.0) - jnp.log1p(jnp.exp(-jnp.abs(ff)))
    lf_ref[...] = lf
    c_ref[...] = _seg_cumsum(lf, seg) + carry_ref[...]
    carry_ref[...] = carry_ref[...] + jnp.sum(lf, axis=1, keepdims=True)


def _proj(x2d, ng, w_main, wff_t, bf, cos_t, sin_t, qg, kg, ones_bd, *, tm, seg, tiles_per_seq):
    n, d_model = x2d.shape
    nt = n // tm
    row = lambda i: (i, 0)
    const = lambda i: (0, 0)
    tab = lambda i: (i % tiles_per_seq, 0)
    col = lambda i: (0, i)
    bf16_out = jax.ShapeDtypeStruct((n, D_RET), BF16)
    f32_out = jax.ShapeDtypeStruct((n, D_FOX), F32)
    t_out = jax.ShapeDtypeStruct((H_F, n), F32)
    blk = pl.BlockSpec((tm, D_RET), row)
    tblk = pl.BlockSpec((H_F, tm), col)
    return pl.pallas_call(
        functools.partial(_proj_kernel, seg=seg, tiles_per_seq=tiles_per_seq),
        grid=(nt,),
        in_specs=[
            pl.BlockSpec((tm, d_model), row),
            pl.BlockSpec((1, d_model), const),
            pl.BlockSpec(w_main.shape, const),
            pl.BlockSpec(wff_t.shape, const),
            pl.BlockSpec((H_F, 1), const),
            pl.BlockSpec((tm, LANES), tab),
            pl.BlockSpec((tm, LANES), tab),
            pl.BlockSpec((1, D_FOX), const),
            pl.BlockSpec((1, D_FOX), const),
            pl.BlockSpec((D_FOX, D_FOX), const),
        ],
        out_specs=[blk, blk, blk, blk, blk, blk, blk, blk, blk, blk, tblk, tblk],
        out_shape=[bf16_out, bf16_out, bf16_out, bf16_out, bf16_out, f32_out, bf16_out, f32_out,
                   bf16_out, bf16_out, t_out, t_out],
        scratch_shapes=[pltpu.VMEM((H_F, 1), F32)],
        compiler_params=_cparams(("arbitrary",)),
        name="proj",
    )(x2d, ng, w_main, wff_t, bf, cos_t, sin_t, qg, kg, ones_bd)


def _ret_kernel(q_ref, k_ref, v_ref, g_ref, s0_ref, dec_ref, qd_ref, kd_ref, sd_ref, rg_ref,
                u_ref, sfin_ref, state_ref, *, chunk, n_chunks):
    j = pl.program_id(1)

    @pl.when(j == 0)
    def _():
        state_ref[...] = s0_ref[...]

    for ci in range(n_chunks):
        rows = slice(ci * chunk, (ci + 1) * chunk)
        for hh in range(H_R):
            cols = slice(hh * DK_R, (hh + 1) * DK_R)
            q = q_ref[rows, cols]
            k = k_ref[rows, cols]
            v = v_ref[rows, cols]
            s = state_ref[hh]
            sc = lax.dot_general(q, k, NT_DIMS, preferred_element_type=F32) * dec_ref[hh]
            o = (jnp.dot(sc.astype(BF16), v, preferred_element_type=F32)
                 + jnp.dot(q, s.astype(BF16), preferred_element_type=F32) * qd_ref[hh])
            kdec = (k.astype(F32) * kd_ref[hh]).astype(BF16)
            state_ref[hh] = (sd_ref[hh] * s
                             + lax.dot_general(kdec, v, TN_DIMS, preferred_element_type=F32))
            ms = jnp.mean(o * o, axis=-1, keepdims=True)
            u = o * lax.rsqrt(ms + EPS) * rg_ref[:, cols] * g_ref[rows, cols].astype(F32)
            u_ref[rows, cols] = u.astype(BF16)

    @pl.when(j == pl.num_programs(1) - 1)
    def _():
        sfin_ref[...] = state_ref[...]


def _retention_tables(chunk, c_valid):
    lg = jnp.log(1.0 - 2.0 ** (-5.0 - jnp.arange(H_R, dtype=F32)))
    idx = jnp.arange(chunk, dtype=F32)
    diff = idx[:, None] - idx[None, :]
    dec = jnp.where(diff >= 0, jnp.exp(jnp.maximum(diff, 0.0)[None] * lg[:, None, None]), 0.0)
    qd = jnp.exp((idx + 1.0)[None, :] * lg[:, None])
    kd = jnp.exp((c_valid - 1.0 - idx)[None, :] * lg[:, None])
    sd = jnp.exp(c_valid * lg)
    qd = jnp.broadcast_to(qd[:, :, None], (H_R, chunk, DK_R))
    kd = jnp.broadcast_to(kd[:, :, None], (H_R, chunk, DK_R))
    sd = jnp.broadcast_to(sd[:, None, None], (H_R, 1, DK_R))
    return dec, qd, kd, sd


def _retention(q, k, v, g, s0, ret_g, *, tile, chunk, c_valid):
    b, t, _ = q.shape
    dec, qd, kd, sd = _retention_tables(chunk, c_valid)
    seq = pl.BlockSpec((None, tile, D_RET), lambda bi, j: (bi, j, 0))
    st = pl.BlockSpec((None, H_R, DK_R, DK_R), lambda bi, j: (bi, 0, 0, 0))

    def full(a):
        return pl.BlockSpec(a.shape, lambda bi, j: (0,) * a.ndim)

    return pl.pallas_call(
        functools.partial(_ret_kernel, chunk=chunk, n_chunks=tile // chunk),
        grid=(b, t // tile),
        in_specs=[seq, seq, seq, seq, st, full(dec), full(qd), full(kd), full(sd), full(ret_g)],
        out_specs=[seq, st],
        out_shape=[jax.ShapeDtypeStruct((b, t, D_RET), BF16),
                   jax.ShapeDtypeStruct((b, H_R, DK_R, DK_R), F32)],
        scratch_shapes=[pltpu.VMEM((H_R, DK_R, DK_R), F32)],
        compiler_params=_cparams(("arbitrary", "arbitrary")),
        name="retention",
    )(q, k, v, g, s0, dec, qd, kd, sd, ret_g)


def _attn_kernel(q_ref, k_ref, v_ref, c_ref, g_ref, o_ref, m_ref, l_ref, acc_ref, *, tq, tk):
    pair = pl.program_id(1)
    i = pl.program_id(2)
    lane = lax.broadcasted_iota(jnp.int32, (tq, LANES), 1)
    first = lane < DH_F
    q = q_ref[...]
    zero = jnp.zeros_like(q)
    qs = (jnp.where(first, q, zero), jnp.where(first, zero, q))

    m_ref[...] = jnp.full(m_ref.shape, NEG, F32)
    l_ref[...] = jnp.zeros_like(l_ref)
    acc_ref[...] = jnp.zeros_like(acc_ref)

    def block(j, mask):
        k0 = pl.multiple_of(j * tk, tk)
        kb = k_ref[pl.ds(k0, tk), :]
        vb = v_ref[pl.ds(k0, tk), :]
        pv = []
        alphas = []
        for hh in range(2):
            ck = c_ref[pl.ds(2 * pair + hh, 1), pl.ds(k0, tk)]
            s = lax.dot_general(qs[hh], kb, NT_DIMS, preferred_element_type=F32) - ck
            if mask is not None:
                s = jnp.where(mask, s, NEG)
            m_prev = m_ref[hh]
            m_new = jnp.maximum(m_prev, jnp.max(s, axis=1, keepdims=True))
            alpha = jnp.exp(m_prev - m_new)
            p = jnp.exp(s - m_new)
            l_ref[hh] = alpha * l_ref[hh] + jnp.sum(p, axis=1, keepdims=True)
            m_ref[hh] = m_new
            pv.append(jnp.dot(p.astype(BF16), vb, preferred_element_type=F32))
            alphas.append(alpha)
        acc_ref[...] = (acc_ref[...] * jnp.where(first, alphas[0], alphas[1])
                        + jnp.where(first, pv[0], pv[1]))

    r = tq // tk
    n_full = i * r

    def body(j, carry):
        block(j, None)
        return carry

    lax.fori_loop(0, n_full, body, 0)

    rowi = lax.broadcasted_iota(jnp.int32, (tq, tk), 0)
    coli = lax.broadcasted_iota(jnp.int32, (tq, tk), 1)
    for jj in range(r):
        block(n_full + jj, rowi >= coli + jj * tk)

    l_full = jnp.where(first, l_ref[0], l_ref[1])
    o_ref[...] = (acc_ref[...] / l_full * g_ref[...].astype(F32)).astype(BF16)


def _attention(fq, fkb, fvb, c_t, sfg, *, batch, seq, tq, tk):
    n = batch * seq
    nq = seq // tq
    qblk = pl.BlockSpec((tq, LANES), lambda b, p, i: (b * nq + i, p))
    kblk = pl.BlockSpec((seq, LANES), lambda b, p, i: (b, p))
    cblk = pl.BlockSpec((H_F, seq), lambda b, p, i: (0, b))
    return pl.pallas_call(
        functools.partial(_attn_kernel, tq=tq, tk=tk),
        grid=(batch, H_F // 2, nq),
        in_specs=[qblk, kblk, kblk, cblk, qblk],
        out_specs=qblk,
        out_shape=jax.ShapeDtypeStruct((n, D_FOX), BF16),
        scratch_shapes=[pltpu.VMEM((2, tq, 1), F32), pltpu.VMEM((2, tq, 1), F32),
                        pltpu.VMEM((tq, LANES), F32)],
        compiler_params=_cparams(("arbitrary", "arbitrary", "arbitrary")),
        name="attention",
    )(fq, fkb, fvb, c_t, sfg)


def _bias_kernel(pt_ref, lf_hbm, out_ref, buf, sem, *, n_pages):
    li = pl.program_id(0)
    b = pl.program_id(1)

    def page_copy(p):
        return pltpu.make_async_copy(lf_hbm.at[li, pt_ref[b, p]], buf.at[p], sem)

    def issue(p, carry):
        page_copy(p).start()
        return carry

    def wait(p, carry):
        page_copy(p).wait()
        return carry

    lax.fori_loop(0, n_pages, issue, 0)
    lax.fori_loop(0, n_pages, wait, 0)

    x = buf[...].reshape(n_pages * H_F, PAGE)
    lane = lax.broadcasted_iota(jnp.int32, x.shape, 1)
    s = x
    d = 1
    while d < PAGE:
        s = s + jnp.where(lane + d < PAGE, pltpu.roll(s, PAGE - d, axis=1), 0.0)
        d *= 2
    s3 = s.reshape(n_pages, H_F, PAGE)
    tot = jnp.broadcast_to(s3[:, :, 0:1], s3.shape)
    e = tot
    d = 1
    while d < n_pages:
        e = e + jnp.concatenate([e[d:], jnp.zeros((d,) + e.shape[1:], F32)], axis=0)
        d *= 2
    res = s3 - buf[...] + (e - tot)
    for p in range(n_pages):
        out_ref[:, p * PAGE:(p + 1) * PAGE] = res[p]


def _past_bias(lf_t, page_table):
    depth = lf_t.shape[0]
    db, n_pages = page_table.shape
    grid_spec = pltpu.PrefetchScalarGridSpec(
        num_scalar_prefetch=1,
        grid=(depth, db),
        in_specs=[pl.BlockSpec(memory_space=pl.ANY)],
        out_specs=pl.BlockSpec((None, None, H_F, n_pages * PAGE), lambda li, b, pt: (li, b, 0, 0)),
        scratch_shapes=[pltpu.VMEM((n_pages, H_F, PAGE), F32), pltpu.SemaphoreType.DMA(())],
    )
    return pl.pallas_call(
        functools.partial(_bias_kernel, n_pages=n_pages),
        grid_spec=grid_spec,
        out_shape=jax.ShapeDtypeStruct((depth, db, H_F, n_pages * PAGE), F32),
        compiler_params=_cparams(("arbitrary", "arbitrary")),
        name="past_bias",
    )(page_table, lf_t)


def _decode_kernel(pt_ref, q_ref, kn_ref, vn_ref, bn_ref, bias_ref, g_ref, *rest, npg, t_new):
    k_refs = rest[:npg]
    v_refs = rest[npg:2 * npg]
    o_ref, m_ref, l_ref, acc_ref = rest[2 * npg:]
    c = pl.program_id(1)
    q = q_ref[...]
    rows = q.shape[0]

    @pl.when(c == 0)
    def _():
        s = lax.dot_general(q, kn_ref[...], NT_DIMS, preferred_element_type=F32) + bn_ref[...]
        m = jnp.max(s, axis=1, keepdims=True)
        p = jnp.exp(s - m)
        m_ref[...] = m
        l_ref[...] = jnp.sum(p, axis=1, keepdims=True)
        acc_ref[...] = jnp.dot(p.astype(BF16), vn_ref[...], preferred_element_type=F32)

    ss = []
    for p in range(npg):
        kp = k_refs[p][...].astype(BF16)
        b8 = bias_ref[:, p * PAGE:(p + 1) * PAGE]
        s = lax.dot_general(q, kp, NT_DIMS, preferred_element_type=F32)
        ss.append(s + jnp.concatenate([b8] * t_new, axis=0))
    m_prev = m_ref[...]
    m_cur = jnp.max(ss[0], axis=1, keepdims=True)
    for p in range(1, npg):
        m_cur = jnp.maximum(m_cur, jnp.max(ss[p], axis=1, keepdims=True))
    m_new = jnp.maximum(m_prev, m_cur)
    alpha = jnp.exp(m_prev - m_new)
    l_new = alpha * l_ref[...]
    acc = alpha * acc_ref[...]
    for p in range(npg):
        pp = jnp.exp(ss[p] - m_new)
        l_new = l_new + jnp.sum(pp, axis=1, keepdims=True)
        acc = acc + jnp.dot(pp.astype(BF16), v_refs[p][...].astype(BF16), preferred_element_type=F32)
    m_ref[...] = m_new
    l_ref[...] = l_new
    acc_ref[...] = acc

    @pl.when(c == pl.num_programs(1) - 1)
    def _():
        o = acc_ref[...] / l_ref[...]
        rowh = lax.broadcasted_iota(jnp.int32, (rows, D_FOX), 0) % H_F
        colh = lax.broadcasted_iota(jnp.int32, (rows, D_FOX), 1) // DH_F
        o = jnp.where(rowh == colh, o, 0.0)
        o4 = jnp.sum(o.reshape(t_new, H_F, D_FOX), axis=1)
        o_ref[...] = o4 * g_ref[...]


def _decode(page_table, qbd, kn, vn, bias_new, bias_past, sfg, cache_k, cache_v, *, layer, npg, t_new):
    db, n_pages = page_table.shape
    rows = qbd.shape[1]
    per_b = lambda a: pl.BlockSpec((None,) + a.shape[1:], lambda b, c, pt: (b,) + (0,) * (a.ndim - 1))

    def page_spec(p):
        return pl.BlockSpec((None, None, PAGE, D_FOX), lambda b, c, pt: (layer, pt[b, c * npg + p], 0, 0))

    grid_spec = pltpu.PrefetchScalarGridSpec(
        num_scalar_prefetch=1,
        grid=(db, n_pages // npg),
        in_specs=([per_b(qbd), per_b(kn), per_b(vn), per_b(bias_new),
                   pl.BlockSpec((None, None, H_F, npg * PAGE), lambda b, c, pt: (layer, b, 0, c)),
                   per_b(sfg)]
                  + [page_spec(p) for p in range(npg)] + [page_spec(p) for p in range(npg)]),
        out_specs=pl.BlockSpec((None, t_new, D_FOX), lambda b, c, pt: (b, 0, 0)),
        scratch_shapes=[pltpu.VMEM((rows, 1), F32), pltpu.VMEM((rows, 1), F32),
                        pltpu.VMEM((rows, D_FOX), F32)],
    )
    return pl.pallas_call(
        functools.partial(_decode_kernel, npg=npg, t_new=t_new),
        grid_spec=grid_spec,
        out_shape=jax.ShapeDtypeStruct((db, t_new, D_FOX), F32),
        compiler_params=_cparams(("arbitrary", "arbitrary")),
        name="decode",
    )(page_table, qbd, kn, vn, bias_new, bias_past, sfg, *([cache_k] * npg), *([cache_v] * npg))


def _merge_kernel(x_ref, ur_ref, uf_ref, w_ref, y_ref):
    y_ref[...] = (x_ref[...]
                  + jnp.dot(ur_ref[...], w_ref[:D_RET, :], preferred_element_type=F32)
                  + jnp.dot(uf_ref[...], w_ref[D_RET:, :], preferred_element_type=F32))


def _merge(x2d, u_ret, u_fox, w_out, *, tm):
    n, d_model = x2d.shape
    row = lambda i: (i, 0)
    return pl.pallas_call(
        _merge_kernel,
        grid=(n // tm,),
        in_specs=[pl.BlockSpec((tm, d_model), row), pl.BlockSpec((tm, D_RET), row),
                  pl.BlockSpec((tm, D_FOX), row), pl.BlockSpec(w_out.shape, lambda i: (0, 0))],
        out_specs=pl.BlockSpec((tm, d_model), row),
        out_shape=jax.ShapeDtypeStruct((n, d_model), F32),
        compiler_params=_cparams(("parallel",)),
        name="merge",
    )(x2d, u_ret, u_fox, w_out)


def _rope_tables(pos):
    inv = 1.0 / (ROPE_BASE ** (jnp.arange(0, DK_R, 2, dtype=F32) / DK_R))
    ang = pos[:, None] * inv[None, :]
    cos = jnp.cos(ang)
    sin = jnp.sin(ang)
    return jnp.concatenate([cos, cos], axis=-1), jnp.concatenate([-sin, sin], axis=-1)


def kernel(x_prompt, x_sample, cache_fox_k, cache_fox_v, cache_fox_logf, state_ret, page_table,
           norm_gain, w_in, ret_norm_gain, fox_q_gain, fox_k_gain, fox_b_f, w_out):
    batch, seq, d_model = x_prompt.shape
    db, t_new, _ = x_sample.shape
    depth = w_in.shape[0]
    n_pool = cache_fox_k.shape[1]
    n_pages = page_table.shape[1]
    past_len = n_pages * PAGE
    n_p = batch * seq
    n_s = db * t_new
    pad_new = 16

    tm_p = min(512, n_p)
    tq = min(256, seq)
    tk = min(256, seq)
    ret_tile = min(512, seq)
    npg = min(8, n_pages)

    w_main = w_in[:, :, :4 * D_RET + 4 * D_FOX].astype(BF16)
    wff_t = jnp.swapaxes(w_in[:, :, 4 * D_RET + 4 * D_FOX:], 1, 2).astype(BF16)
    w_out_b = w_out.astype(BF16)
    head_id = jnp.arange(D_FOX, dtype=jnp.int32) // DH_F
    ones_bd = (head_id[:, None] == head_id[None, :]).astype(BF16)
    cos_p, sin_p = _rope_tables(jnp.arange(seq, dtype=F32))
    pos_s = past_len + jnp.arange(t_new, dtype=F32)
    cos_s, sin_s = _rope_tables(jnp.tile(pos_s, db))
    qg = jnp.tile(fox_q_gain, (1, H_F)) * (DH_F ** -0.5)
    kg = jnp.tile(fox_k_gain, (1, H_F))
    ret_g = ret_norm_gain.reshape(depth, 1, D_RET)

    cache_k = cache_fox_k.reshape(depth, n_pool, PAGE, D_FOX)
    cache_v = cache_fox_v.reshape(depth, n_pool, PAGE, D_FOX)
    lf_t = jnp.swapaxes(cache_fox_logf, 2, 3)
    bias_past = _past_bias(lf_t, page_table)

    eye_h = jnp.eye(H_F, dtype=BF16)
    jj = jnp.arange(pad_new)
    new_valid = (jj[None, :] <= jnp.arange(t_new)[:, None]) & (jj[None, :] < t_new)
    s0_p = jnp.zeros((batch, H_R, DK_R, DK_R), F32)

    def pad_rows(a):
        a = a.reshape(db, t_new, a.shape[-1])
        return jnp.pad(a, ((0, 0), (0, pad_new - t_new), (0, 0)))

    xp = x_prompt.reshape(n_p, d_model)
    xs = x_sample.reshape(n_s, d_model)
    outs = [[] for _ in range(8)]
    for l in range(depth):
        proj_args = (norm_gain[l][None, :], w_main[l], wff_t[l], fox_b_f[l][:, None])
        gains = (qg[l][None, :], kg[l][None, :], ones_bd)

        (rq, rk, rv, srg, fq, fk, fkb, fv, fvb, sfg, lf, c_t) = _proj(
            xp, *proj_args, cos_p, sin_p, *gains, tm=tm_p, seg=tm_p, tiles_per_seq=seq // tm_p)
        r3 = lambda a: a.reshape(batch, seq, D_RET)
        u_ret, s_p = _retention(r3(rq), r3(rk), r3(rv), r3(srg), s0_p, ret_g[l],
                                tile=ret_tile, chunk=min(RET_CHUNK, seq), c_valid=float(min(RET_CHUNK, seq)))
        u_fox = _attention(fq, fkb, fvb, c_t, sfg, batch=batch, seq=seq, tq=tq, tk=tk)
        xp = _merge(xp, u_ret.reshape(n_p, D_RET), u_fox, w_out_b[l], tm=tm_p)
        outs[0].append(fk.reshape(batch, seq, H_F, DH_F))
        outs[1].append(fv.reshape(batch, seq, H_F, DH_F))
        outs[2].append(lf.T.reshape(batch, seq, H_F))
        outs[3].append(s_p)

        (rq, rk, rv, srg, fq, fk, fkb, fv, fvb, sfg, lf, c_t) = _proj(
            xs, *proj_args, cos_s, sin_s, *gains, tm=n_s, seg=t_new, tiles_per_seq=1)
        u_ret, s_s = _retention(pad_rows(rq), pad_rows(rk), pad_rows(rv), pad_rows(srg), state_ret[l], ret_g[l],
                                tile=pad_new, chunk=pad_new, c_valid=float(t_new))
        u_ret = u_ret[:, :t_new].reshape(n_s, D_RET)
        q5 = fq.reshape(db, t_new, 1, H_F, DH_F) * eye_h[None, None, :, :, None]
        qbd = q5.reshape(db, t_new * H_F, D_FOX)
        cn = c_t.reshape(H_F, db, t_new).transpose(1, 2, 0)
        bn = jnp.pad(-cn.transpose(0, 2, 1), ((0, 0), (0, 0), (0, pad_new - t_new)))
        bn = jnp.where(new_valid[None, :, None, :], bn[:, None, :, :], NEG)
        bn = bn.reshape(db, t_new * H_F, pad_new)
        u_fox = _decode(page_table, qbd, pad_rows(fkb), pad_rows(fvb), bn, bias_past,
                        sfg.astype(F32).reshape(db, t_new, D_FOX), cache_k, cache_v,
                        layer=l, npg=npg, t_new=t_new)
        xs = _merge(xs, u_ret, u_fox.reshape(n_s, D_FOX).astype(BF16), w_out_b[l], tm=n_s)
        outs[4].append(fk.reshape(db, t_new, H_F, DH_F))
        outs[5].append(fv.reshape(db, t_new, H_F, DH_F))
        outs[6].append(lf.T.reshape(db, t_new, H_F))
        outs[7].append(s_s)

    return (xp.reshape(batch, seq, d_model), xs.reshape(db, t_new, d_model),
            jnp.stack(outs[0]), jnp.stack(outs[1]), jnp.stack(outs[2]), jnp.stack(outs[3]),
            jnp.stack(outs[4]), jnp.stack(outs[5]), jnp.stack(outs[6]), jnp.stack(outs[7]))
```

```python
import functools
import math

import jax
import jax.numpy as jnp
from jax import lax
from jax.experimental import pallas as pl
from jax.experimental.pallas import tpu as pltpu

F32 = jnp.float32
BF16 = jnp.bfloat16

H_R = 4
DK_R = 128
H_F = 8
DH_F = 64
D_RET = H_R * DK_R
D_FOX = H_F * DH_F
PAGE = 128
RET_CHUNK = 128
ROPE_BASE = 10000.0
EPS = 1e-6
NEG = -1e30
LOG2E = math.log2(math.e)

LANES = 128
SUBLANES = 8
VMEM_LIMIT = 56 * 1024 * 1024

NT_DIMS = (((1,), (1,)), ((), ()))
TN_DIMS = (((0,), (0,)), ((), ()))


def _cparams(sem):
    return pltpu.CompilerParams(dimension_semantics=sem, vmem_limit_bytes=VMEM_LIMIT)


def _seg_cumsum(x, seg):
    lane = lax.broadcasted_iota(jnp.int32, x.shape, 1)
    pos = lane & (seg - 1)
    d = 1
    while d < seg:
        shifted = pltpu.roll(x, d, axis=1)
        x = x + jnp.where(pos >= d, shifted, 0.0)
        d *= 2
    return x


def _proj_kernel(x_ref, ng_ref, w_ref, wff_ref, bf_ref, cos_ref, sin_ref, qg_ref, kg_ref, ones_ref,
                 rq_ref, rk_ref, rv_ref, srg_ref, fq_ref, fk_ref, fkb_ref, fv_ref, fvb_ref, sfg_ref,
                 lf_ref, c_ref, carry_ref, *, seg, tiles_per_seq):
    i = pl.program_id(0)

    @pl.when(i % tiles_per_seq == 0)
    def _():
        carry_ref[...] = jnp.zeros_like(carry_ref)

    x = x_ref[...]
    ms = jnp.mean(x * x, axis=-1, keepdims=True)
    h = (x * lax.rsqrt(ms + EPS) * ng_ref[...]).astype(BF16)

    def zcol(g):
        return jnp.dot(h, w_ref[:, g * D_RET:(g + 1) * D_RET], preferred_element_type=F32)

    cos = cos_ref[...]
    sin = sin_ref[...]

    def rope_store(z, out_ref, scale):
        for hh in range(H_R):
            zz = z[:, hh * DK_R:(hh + 1) * DK_R]
            rot = pltpu.roll(zz, DK_R // 2, axis=1)
            r = zz * cos + rot * sin
            if scale is not None:
                r = r * scale
            out_ref[:, hh * DK_R:(hh + 1) * DK_R] = r.astype(out_ref.dtype)

    def silu(z):
        return z / (1.0 + jnp.exp(-z))

    def head_norm(z, gain):
        y = z * z
        yh = y.astype(BF16)
        yl = (y - yh.astype(F32)).astype(BF16)
        ssq = (jnp.dot(yh, ones_ref[...], preferred_element_type=F32)
               + jnp.dot(yl, ones_ref[...], preferred_element_type=F32))
        return z * lax.rsqrt(ssq * (1.0 / DH_F) + EPS) * gain

    rope_store(zcol(0), rq_ref, None)
    rope_store(zcol(1), rk_ref, DK_R ** -0.5)
    rv_ref[...] = zcol(2).astype(BF16)
    srg_ref[...] = silu(zcol(3)).astype(BF16)
    fq_ref[...] = head_norm(zcol(4), qg_ref[...]).astype(BF16)
    fk = head_norm(zcol(5), kg_ref[...])
    fk_ref[...] = fk
    fkb_ref[...] = fk.astype(BF16)
    fv = zcol(6)
    fv_ref[...] = fv
    fvb_ref[...] = fv.astype(BF16)
    sfg_ref[...] = silu(zcol(7)).astype(BF16)

    ff = lax.dot_general(wff_ref[...], h, NT_DIMS, preferred_element_type=F32) + bf_ref[...]
    lf = jnp.minimum(ff, 0.0) - jnp.log1p(jnp.exp(-jnp.abs(ff)))
    lf_ref[...] = lf
    c_ref[...] = _seg_cumsum(lf, seg) + carry_ref[...]
    carry_ref[...] = carry_ref[...] + jnp.sum(lf, axis=1, keepdims=True)


def _proj(x2d, ng, w_main, wff_t, bf, cos_t, sin_t, qg, kg, ones_bd, *, tm, seg, tiles_per_seq):
    n, d_model = x2d.shape
    nt = n // tm
    row = lambda i: (i, 0)
    const = lambda i: (0, 0)
    tab = lambda i: (i % tiles_per_seq, 0)
    col = lambda i: (0, i)
    bf16_out = jax.ShapeDtypeStruct((n, D_RET), BF16)
    f32_out = jax.ShapeDtypeStruct((n, D_FOX), F32)
    t_out = jax.ShapeDtypeStruct((H_F, n), F32)
    blk = pl.BlockSpec((tm, D_RET), row)
    tblk = pl.BlockSpec((H_F, tm), col)
    return pl.pallas_call(
        functools.partial(_proj_kernel, seg=seg, tiles_per_seq=tiles_per_seq),
        grid=(nt,),
        in_specs=[
            pl.BlockSpec((tm, d_model), row),
            pl.BlockSpec((1, d_model), const),
            pl.BlockSpec(w_main.shape, const),
            pl.BlockSpec(wff_t.shape, const),
            pl.BlockSpec((H_F, 1), const),
            pl.BlockSpec((tm, LANES), tab),
            pl.BlockSpec((tm, LANES), tab),
            pl.BlockSpec((1, D_FOX), const),
            pl.BlockSpec((1, D_FOX), const),
            pl.BlockSpec((D_FOX, D_FOX), const),
        ],
        out_specs=[blk, blk, blk, blk, blk, blk, blk, blk, blk, blk, tblk, tblk],
        out_shape=[bf16_out, bf16_out, bf16_out, bf16_out, bf16_out, f32_out, bf16_out, f32_out,
                   bf16_out, bf16_out, t_out, t_out],
        scratch_shapes=[pltpu.VMEM((H_F, 1), F32)],
        compiler_params=_cparams(("arbitrary",)),
        name="proj",
    )(x2d, ng, w_main, wff_t, bf, cos_t, sin_t, qg, kg, ones_bd)


def _ret_kernel(q_ref, k_ref, v_ref, g_ref, s0_ref, dec_ref, qd_ref, kd_ref, sd_ref, rg_ref,
                u_ref, sfin_ref, state_ref, *, chunk, n_chunks):
    j = pl.program_id(1)

    @pl.when(j == 0)
    def _():
        state_ref[...] = s0_ref[...]

    for ci in range(n_chunks):
        rows = slice(ci * chunk, (ci + 1) * chunk)
        for hh in range(H_R):
            cols = slice(hh * DK_R, (hh + 1) * DK_R)
            q = q_ref[rows, cols]
            k = k_ref[rows, cols]
            v = v_ref[rows, cols]
            s = state_ref[hh]
            sc = lax.dot_general(q, k, NT_DIMS, preferred_element_type=F32) * dec_ref[hh]
            o = (jnp.dot(sc.astype(BF16), v, preferred_element_type=F32)
                 + jnp.dot(q, s.astype(BF16), preferred_element_type=F32) * qd_ref[hh])
            kdec = (k.astype(F32) * kd_ref[hh]).astype(BF16)
            state_ref[hh] = (sd_ref[hh] * s
                             + lax.dot_general(kdec, v, TN_DIMS, preferred_element_type=F32))
            ms = jnp.mean(o * o, axis=-1, keepdims=True)
            u = o * lax.rsqrt(ms + EPS) * rg_ref[:, cols] * g_ref[rows, cols].astype(F32)
            u_ref[rows, cols] = u.astype(BF16)

    @pl.when(j == pl.num_programs(1) - 1)
    def _():
        sfin_ref[...] = state_ref[...]


def _retention_tables(chunk, c_valid):
    lg = jnp.log(1.0 - 2.0 ** (-5.0 - jnp.arange(H_R, dtype=F32)))
    idx = jnp.arange(chunk, dtype=F32)
    diff = idx[:, None] - idx[None, :]
    dec = jnp.where(diff >= 0, jnp.exp(jnp.maximum(diff, 0.0)[None] * lg[:, None, None]), 0.0)
    qd = jnp.exp((idx + 1.0)[None, :] * lg[:, None])
    kd = jnp.exp((c_valid - 1.0 - idx)[None, :] * lg[:, None])
    sd = jnp.exp(c_valid * lg)
    qd = jnp.broadcast_to(qd[:, :, None], (H_R, chunk, DK_R))
    kd = jnp.broadcast_to(kd[:, :, None], (H_R, chunk, DK_R))
    sd = jnp.broadcast_to(sd[:, None, None], (H_R, 1, DK_R))
    return dec, qd, kd, sd


def _retention(q, k, v, g, s0, ret_g, *, tile, chunk, c_valid):
    b, t, _ = q.shape
    dec, qd, kd, sd = _retention_tables(chunk, c_valid)
    seq = pl.BlockSpec((None, tile, D_RET), lambda bi, j: (bi, j, 0))
    st = pl.BlockSpec((None, H_R, DK_R, DK_R), lambda bi, j: (bi, 0, 0, 0))

    def full(a):
        return pl.BlockSpec(a.shape, lambda bi, j: (0,) * a.ndim)

    return pl.pallas_call(
        functools.partial(_ret_kernel, chunk=chunk, n_chunks=tile // chunk),
        grid=(b, t // tile),
        in_specs=[seq, seq, seq, seq, st, full(dec), full(qd), full(kd), full(sd), full(ret_g)],
        out_specs=[seq, st],
        out_shape=[jax.ShapeDtypeStruct((b, t, D_RET), BF16),
                   jax.ShapeDtypeStruct((b, H_R, DK_R, DK_R), F32)],
        scratch_shapes=[pltpu.VMEM((H_R, DK_R, DK_R), F32)],
        compiler_params=_cparams(("arbitrary", "arbitrary")),
        name="retention",
    )(q, k, v, g, s0, dec, qd, kd, sd, ret_g)


def _attn_kernel(q_ref, k_ref, v_ref, c_ref, g_ref, o_ref, m_ref, acc_ref, *, tq, tk, seq):
    pair = pl.program_id(1)
    nchunk = tk // LANES
    first_q = lax.broadcasted_iota(jnp.int32, (tq, LANES), 1) < DH_F
    first_k = lax.broadcasted_iota(jnp.int32, (tk, LANES), 1) < DH_F
    rowi = lax.broadcasted_iota(jnp.int32, (tq, LANES), 0)
    coli = lax.broadcasted_iota(jnp.int32, (tq, LANES), 1)

    def q_block(i, carry):
        q0 = pl.multiple_of(i * tq, tq)
        q = q_ref[pl.ds(q0, tq), :]
        zero = jnp.zeros_like(q)
        qs = (jnp.where(first_q, q, zero), jnp.where(first_q, zero, q))
        m_ref[...] = jnp.full(m_ref.shape, NEG, F32)
        acc_ref[...] = jnp.zeros_like(acc_ref)

        def kv_block(j, diag_offset):
            k0 = pl.multiple_of(j * tk, tk)
            kb = k_ref[pl.ds(k0, tk), :]
            vb = v_ref[pl.ds(k0, tk), :]
            one = jnp.ones_like(vb)
            vs = (jnp.where(first_k, vb, one), jnp.where(first_k, one, vb))
            for hh in range(2):
                ck = c_ref[pl.ds(2 * pair + hh, 1), pl.ds(k0, tk)] * LOG2E
                s = lax.dot_general(qs[hh], kb, NT_DIMS, preferred_element_type=F32)
                chunks = []
                for c in range(nchunk):
                    ch = s[:, c * LANES:(c + 1) * LANES] - ck[:, c * LANES:(c + 1) * LANES]
                    if diag_offset is not None:
                        ch = jnp.where(rowi >= coli + (diag_offset + c * LANES), ch, NEG)
                    chunks.append(ch)
                mx = chunks[0]
                for c in range(1, nchunk):
                    mx = jnp.maximum(mx, chunks[c])
                m_prev = m_ref[hh]
                m_new = jnp.maximum(m_prev, jnp.max(mx, axis=1, keepdims=True))
                alpha = jnp.exp2(m_prev - m_new)
                p = jnp.concatenate([jnp.exp2(ch - m_new) for ch in chunks], axis=1).astype(BF16)
                acc_ref[hh] = acc_ref[hh] * alpha + jnp.dot(p, vs[hh], preferred_element_type=F32)
                m_ref[hh] = m_new

        r = tq // tk
        n_full = i * r

        def body(j, c2):
            kv_block(j, None)
            return c2

        lax.fori_loop(0, n_full, body, 0)
        for jj in range(r):
            kv_block(n_full + jj, jj * tk)

        acc_a = acc_ref[0]
        acc_b = acc_ref[1]
        num = jnp.where(first_q, acc_a, acc_b)
        den = jnp.where(first_q, pltpu.roll(acc_a, DH_F, axis=1), pltpu.roll(acc_b, DH_F, axis=1))
        o_ref[pl.ds(q0, tq), :] = (num / den * g_ref[pl.ds(q0, tq), :].astype(F32)).astype(BF16)
        return carry

    lax.fori_loop(0, seq // tq, q_block, 0)


def _attention(fq, fkb, fvb, c_t, sfg, *, batch, seq, tq, tk):
    n = batch * seq
    blk = pl.BlockSpec((seq, LANES), lambda b, p: (b, p))
    cblk = pl.BlockSpec((H_F, seq), lambda b, p: (0, b))
    return pl.pallas_call(
        functools.partial(_attn_kernel, tq=tq, tk=tk, seq=seq),
        grid=(batch, H_F // 2),
        in_specs=[blk, blk, blk, cblk, blk],
        out_specs=blk,
        out_shape=jax.ShapeDtypeStruct((n, D_FOX), BF16),
        scratch_shapes=[pltpu.VMEM((2, tq, LANES), F32), pltpu.VMEM((2, tq, LANES), F32)],
        compiler_params=_cparams(("arbitrary", "arbitrary")),
        name="attention",
    )(fq, fkb, fvb, c_t, sfg)


def _bias_kernel(pt_ref, lf_hbm, out_ref, buf, sem, *, n_pages):
    li = pl.program_id(0)
    b = pl.program_id(1)

    def page_copy(p):
        return pltpu.make_async_copy(lf_hbm.at[li, pt_ref[b, p]], buf.at[p], sem)

    def issue(p, carry):
        page_copy(p).start()
        return carry

    def wait(p, carry):
        page_copy(p).wait()
        return carry

    lax.fori_loop(0, n_pages, issue, 0)
    lax.fori_loop(0, n_pages, wait, 0)

    n = n_pages * SUBLANES
    x = buf[...].reshape(n, LANES)
    lane = lax.broadcasted_iota(jnp.int32, x.shape, 1)
    sub = lax.broadcasted_iota(jnp.int32, x.shape, 0) & (SUBLANES - 1)
    s = x
    t = x
    d = H_F
    while d < LANES:
        s = s + jnp.where(lane + d < LANES, pltpu.roll(s, LANES - d, axis=1), 0.0)
        t = t + pltpu.roll(t, d, axis=1)
        d *= 2
    u = t
    d = 1
    while d < SUBLANES:
        u = u + jnp.where(sub + d < SUBLANES, pltpu.roll(u, n - d, axis=0), 0.0)
        d *= 2
    u3 = u.reshape(n_pages, SUBLANES, LANES)
    tot = jnp.broadcast_to(u3[:, 0:1, :], u3.shape)
    e = tot
    d = 1
    while d < n_pages:
        e = e + jnp.concatenate([e[d:], jnp.zeros((d,) + e.shape[1:], F32)], axis=0)
        d *= 2
    out_ref[...] = (s - x + u - t).reshape(n_pages, SUBLANES, LANES) + (e - tot)


def _past_bias(lf_flat, page_table):
    depth = lf_flat.shape[0]
    db, n_pages = page_table.shape
    grid_spec = pltpu.PrefetchScalarGridSpec(
        num_scalar_prefetch=1,
        grid=(depth, db),
        in_specs=[pl.BlockSpec(memory_space=pl.ANY)],
        out_specs=pl.BlockSpec((None, None, n_pages, SUBLANES, LANES), lambda li, b, pt: (li, b, 0, 0, 0)),
        scratch_shapes=[pltpu.VMEM((n_pages, SUBLANES, LANES), F32), pltpu.SemaphoreType.DMA(())],
    )
    return pl.pallas_call(
        functools.partial(_bias_kernel, n_pages=n_pages),
        grid_spec=grid_spec,
        out_shape=jax.ShapeDtypeStruct((depth, db, n_pages, SUBLANES, LANES), F32),
        compiler_params=_cparams(("arbitrary", "arbitrary")),
        name="past_bias",
    )(page_table, lf_flat)


def _decode_kernel(pt_ref, q_ref, kn_ref, vn_ref, bn_ref, r_ref, g_ref, *rest, npg):
    k_refs = rest[:npg]
    v_refs = rest[npg:2 * npg]
    o_ref, m_ref, l_ref, acc_ref, s_ref = rest[2 * npg:]
    c = pl.program_id(1)
    q = q_ref[...]
    rows = q.shape[0]
    flat = PAGE * H_F
    nchunk = flat // LANES

    @pl.when(c == 0)
    def _():
        s = lax.dot_general(q, kn_ref[...], NT_DIMS, preferred_element_type=F32) + bn_ref[...]
        m = jnp.max(s, axis=1, keepdims=True)
        p = jnp.exp2(s - m)
        m_ref[...] = m
        l_ref[...] = jnp.sum(p, axis=1, keepdims=True)
        acc_ref[...] = jnp.dot(p.astype(BF16), vn_ref[...], preferred_element_type=F32)

    same_head = ((lax.broadcasted_iota(jnp.int32, (rows, LANES), 1) & (H_F - 1))
                 == (lax.broadcasted_iota(jnp.int32, (rows, LANES), 0) & (H_F - 1)))
    mx = jnp.full((rows, LANES), NEG, F32)
    for p in range(npg):
        kf = k_refs[p][...].reshape(flat, DH_F).astype(BF16)
        s = lax.dot_general(q, kf, NT_DIMS, preferred_element_type=F32)
        r = r_ref[p] * LOG2E
        for a in range(nchunk):
            rb = jnp.broadcast_to(r[a:a + 1, :], (rows, LANES))
            ch = jnp.where(same_head, s[:, a * LANES:(a + 1) * LANES] + rb, NEG)
            s_ref[:, (p * nchunk + a) * LANES:(p * nchunk + a + 1) * LANES] = ch
            mx = jnp.maximum(mx, ch)
    m_prev = m_ref[...]
    m_new = jnp.maximum(m_prev, jnp.max(mx, axis=1, keepdims=True))
    alpha = jnp.exp2(m_prev - m_new)
    acc = alpha * acc_ref[...]
    lsum = jnp.zeros((rows, LANES), F32)
    for p in range(npg):
        pp = jnp.exp2(s_ref[:, p * flat:(p + 1) * flat] - m_new)
        for a in range(nchunk):
            lsum = lsum + pp[:, a * LANES:(a + 1) * LANES]
        vf = v_refs[p][...].reshape(flat, DH_F).astype(BF16)
        acc = acc + jnp.dot(pp.astype(BF16), vf, preferred_element_type=F32)
    m_ref[...] = m_new
    l_ref[...] = alpha * l_ref[...] + jnp.sum(lsum, axis=1, keepdims=True)
    acc_ref[...] = acc

    @pl.when(c == pl.num_programs(1) - 1)
    def _():
        o_ref[...] = acc_ref[...] / l_ref[...] * g_ref[...]


def _decode(page_table, q, kn, vn, bias_new, bias_past, sfg, cache_k, cache_v, *, layer, npg):
    db, n_pages = page_table.shape
    rows = q.shape[1]
    per_b = lambda a: pl.BlockSpec((None,) + a.shape[1:], lambda b, c, pt: (b,) + (0,) * (a.ndim - 1))

    def page_spec(p):
        return pl.BlockSpec((None, None, PAGE, H_F, DH_F),
                            lambda b, c, pt: (layer, pt[b, c * npg + p], 0, 0, 0))

    grid_spec = pltpu.PrefetchScalarGridSpec(
        num_scalar_prefetch=1,
        grid=(db, n_pages // npg),
        in_specs=([per_b(q), per_b(kn), per_b(vn), per_b(bias_new),
                   pl.BlockSpec((None, None, npg, SUBLANES, LANES), lambda b, c, pt: (layer, b, c, 0, 0)),
                   per_b(sfg)]
                  + [page_spec(p) for p in range(npg)] + [page_spec(p) for p in range(npg)]),
        out_specs=pl.BlockSpec((None, rows, DH_F), lambda b, c, pt: (b, 0, 0)),
        scratch_shapes=[pltpu.VMEM((rows, 1), F32), pltpu.VMEM((rows, 1), F32),
                        pltpu.VMEM((rows, DH_F), F32), pltpu.VMEM((rows, npg * PAGE * H_F), F32)],
    )
    return pl.pallas_call(
        functools.partial(_decode_kernel, npg=npg),
        grid_spec=grid_spec,
        out_shape=jax.ShapeDtypeStruct((db, rows, DH_F), F32),
        compiler_params=_cparams(("arbitrary", "arbitrary")),
        name="decode",
    )(page_table, q, kn, vn, bias_new, bias_past, sfg, *([cache_k] * npg), *([cache_v] * npg))


def _merge_kernel(x_ref, ur_ref, uf_ref, w_ref, y_ref):
    y_ref[...] = (x_ref[...]
                  + jnp.dot(ur_ref[...], w_ref[:D_RET, :], preferred_element_type=F32)
                  + jnp.dot(uf_ref[...], w_ref[D_RET:, :], preferred_element_type=F32))


def _merge(x2d, u_ret, u_fox, w_out, *, tm):
    n, d_model = x2d.shape
    row = lambda i: (i, 0)
    return pl.pallas_call(
        _merge_kernel,
        grid=(n // tm,),
        in_specs=[pl.BlockSpec((tm, d_model), row), pl.BlockSpec((tm, D_RET), row),
                  pl.BlockSpec((tm, D_FOX), row), pl.BlockSpec(w_out.shape, lambda i: (0, 0))],
        out_specs=pl.BlockSpec((tm, d_model), row),
        out_shape=jax.ShapeDtypeStruct((n, d_model), F32),
        compiler_params=_cparams(("parallel",)),
        name="merge",
    )(x2d, u_ret, u_fox, w_out)


def _rope_tables(pos):
    inv = 1.0 / (ROPE_BASE ** (jnp.arange(0, DK_R, 2, dtype=F32) / DK_R))
    ang = pos[:, None] * inv[None, :]
    cos = jnp.cos(ang)
    sin = jnp.sin(ang)
    return jnp.concatenate([cos, cos], axis=-1), jnp.concatenate([-sin, sin], axis=-1)


def kernel(x_prompt, x_sample, cache_fox_k, cache_fox_v, cache_fox_logf, state_ret, page_table,
           norm_gain, w_in, ret_norm_gain, fox_q_gain, fox_k_gain, fox_b_f, w_out):
    batch, seq, d_model = x_prompt.shape
    db, t_new, _ = x_sample.shape
    depth = w_in.shape[0]
    n_pool = cache_fox_k.shape[1]
    n_pages = page_table.shape[1]
    past_len = n_pages * PAGE
    n_p = batch * seq
    n_s = db * t_new
    pad_new = 16
    rows_s = t_new * H_F

    tm_p = min(512, n_p)
    tq = min(512, seq)
    tk = min(512, seq)
    ret_tile = min(512, seq)
    npg = min(8, n_pages)

    w_main = w_in[:, :, :4 * D_RET + 4 * D_FOX].astype(BF16)
    wff_t = jnp.swapaxes(w_in[:, :, 4 * D_RET + 4 * D_FOX:], 1, 2).astype(BF16)
    w_out_b = w_out.astype(BF16)
    head_id = jnp.arange(D_FOX, dtype=jnp.int32) // DH_F
    ones_bd = (head_id[:, None] == head_id[None, :]).astype(BF16)
    cos_p, sin_p = _rope_tables(jnp.arange(seq, dtype=F32))
    pos_s = past_len + jnp.arange(t_new, dtype=F32)
    cos_s, sin_s = _rope_tables(jnp.tile(pos_s, db))
    qg = jnp.tile(fox_q_gain, (1, H_F)) * (DH_F ** -0.5 * LOG2E)
    kg = jnp.tile(fox_k_gain, (1, H_F))
    ret_g = ret_norm_gain.reshape(depth, 1, D_RET)

    lf_flat = cache_fox_logf.reshape(depth, n_pool, SUBLANES, PAGE * H_F // SUBLANES)
    bias_past = _past_bias(lf_flat, page_table)

    tt = jnp.arange(rows_s) // H_F
    hh = jnp.arange(rows_s) % H_F
    new_valid = (hh[:, None] == hh[None, :]) & (tt[None, :] <= tt[:, None])
    s0_p = jnp.zeros((batch, H_R, DK_R, DK_R), F32)

    def pad_rows(a):
        a = a.reshape(db, t_new, a.shape[-1])
        return jnp.pad(a, ((0, 0), (0, pad_new - t_new), (0, 0)))

    def head_rows(a):
        return a.reshape(db, rows_s, DH_F)

    xp = x_prompt.reshape(n_p, d_model)
    xs = x_sample.reshape(n_s, d_model)
    outs = [[] for _ in range(8)]
    for l in range(depth):
        proj_args = (norm_gain[l][None, :], w_main[l], wff_t[l], fox_b_f[l][:, None])
        gains = (qg[l][None, :], kg[l][None, :], ones_bd)

        (rq, rk, rv, srg, fq, fk, fkb, fv, fvb, sfg, lf, c_t) = _proj(
            xp, *proj_args, cos_p, sin_p, *gains, tm=tm_p, seg=tm_p, tiles_per_seq=seq // tm_p)
        r3 = lambda a: a.reshape(batch, seq, D_RET)
        u_ret, s_p = _retention(r3(rq), r3(rk), r3(rv), r3(srg), s0_p, ret_g[l],
                                tile=ret_tile, chunk=min(RET_CHUNK, seq), c_valid=float(min(RET_CHUNK, seq)))
        u_fox = _attention(fq, fkb, fvb, c_t, sfg, batch=batch, seq=seq, tq=tq, tk=tk)
        xp = _merge(xp, u_ret.reshape(n_p, D_RET), u_fox, w_out_b[l], tm=tm_p)
        outs[0].append(fk.reshape(batch, seq, H_F, DH_F))
        outs[1].append(fv.reshape(batch, seq, H_F, DH_F))
        outs[2].append(lf.T.reshape(batch, seq, H_F))
        outs[3].append(s_p)

        (rq, rk, rv, srg, fq, fk, fkb, fv, fvb, sfg, lf, c_t) = _proj(
            xs, *proj_args, cos_s, sin_s, *gains, tm=n_s, seg=t_new, tiles_per_seq=1)
        u_ret, s_s = _retention(pad_rows(rq), pad_rows(rk), pad_rows(rv), pad_rows(srg), state_ret[l], ret_g[l],
                                tile=pad_new, chunk=pad_new, c_valid=float(t_new))
        u_ret = u_ret[:, :t_new].reshape(n_s, D_RET)
        cn = c_t.reshape(H_F, db, t_new).transpose(1, 2, 0).reshape(db, 1, rows_s)
        bn = jnp.where(new_valid[None], -cn * LOG2E, NEG)
        u_fox = _decode(page_table, head_rows(fq), head_rows(fkb), head_rows(fvb), bn, bias_past,
                        head_rows(sfg.astype(F32)), cache_fox_k, cache_fox_v, layer=l, npg=npg)
        xs = _merge(xs, u_ret, u_fox.reshape(n_s, D_FOX).astype(BF16), w_out_b[l], tm=n_s)
        outs[4].append(fk.reshape(db, t_new, H_F, DH_F))
        outs[5].append(fv.reshape(db, t_new, H_F, DH_F))
        outs[6].append(lf.T.reshape(db, t_new, H_F))
        outs[7].append(s_s)

    return (xp.reshape(batch, seq, d_model), xs.reshape(db, t_new, d_model),
            jnp.stack(outs[0]), jnp.stack(outs[1]), jnp.stack(outs[2]), jnp.stack(outs[3]),
            jnp.stack(outs[4]), jnp.stack(outs[5]), jnp.stack(outs[6]), jnp.stack(outs[7]))
```

```python
import functools
import math

import jax
import jax.numpy as jnp
from jax import lax
from jax.experimental import pallas as pl
from jax.experimental.pallas import tpu as pltpu

F32 = jnp.float32
BF16 = jnp.bfloat16

H_R = 4
DK_R = 128
H_F = 8
DH_F = 64
D_RET = H_R * DK_R
D_FOX = H_F * DH_F
PAGE = 128
RET_CHUNK = 128
ROPE_BASE = 10000.0
EPS = 1e-6
NEG = -1e30
LOG2E = math.log2(math.e)

LANES = 128
SUBLANES = 8
VMEM_LIMIT = 56 * 1024 * 1024

NT_DIMS = (((1,), (1,)), ((), ()))
TN_DIMS = (((0,), (0,)), ((), ()))


def _cparams(sem):
    return pltpu.CompilerParams(dimension_semantics=sem, vmem_limit_bytes=VMEM_LIMIT)


def _seg_cumsum(x, seg):
    lane = lax.broadcasted_iota(jnp.int32, x.shape, 1)
    pos = lane & (seg - 1)
    d = 1
    while d < seg:
        shifted = pltpu.roll(x, d, axis=1)
        x = x + jnp.where(pos >= d, shifted, 0.0)
        d *= 2
    return x


def _proj_kernel(x_ref, ng_ref, w_ref, wff_ref, bf_ref, cos_ref, sin_ref, qg_ref, kg_ref, ones_ref,
                 rq_ref, rk_ref, rv_ref, srg_ref, fq_ref, fk_ref, fkb_ref, fv_ref, fvb_ref, sfg_ref,
                 lf_ref, c_ref, carry_ref, *, seg, tiles_per_seq):
    i = pl.program_id(0)

    @pl.when(i % tiles_per_seq == 0)
    def _():
        carry_ref[...] = jnp.zeros_like(carry_ref)

    x = x_ref[...]
    ms = jnp.mean(x * x, axis=-1, keepdims=True)
    h = (x * lax.rsqrt(ms + EPS) * ng_ref[...]).astype(BF16)

    def zcol(g):
        return jnp.dot(h, w_ref[:, g * D_RET:(g + 1) * D_RET], preferred_element_type=F32)

    cos = cos_ref[...]
    sin = sin_ref[...]

    def rope_store(z, out_ref, scale):
        for hh in range(H_R):
            zz = z[:, hh * DK_R:(hh + 1) * DK_R]
            rot = pltpu.roll(zz, DK_R // 2, axis=1)
            r = zz * cos + rot * sin
            if scale is not None:
                r = r * scale
            out_ref[:, hh * DK_R:(hh + 1) * DK_R] = r.astype(out_ref.dtype)

    def silu(z):
        return z / (1.0 + jnp.exp(-z))

    def head_norm(z, gain):
        y = z * z
        yh = y.astype(BF16)
        yl = (y - yh.astype(F32)).astype(BF16)
        ssq = (jnp.dot(yh, ones_ref[...], preferred_element_type=F32)
               + jnp.dot(yl, ones_ref[...], preferred_element_type=F32))
        return z * lax.rsqrt(ssq * (1.0 / DH_F) + EPS) * gain

    rope_store(zcol(0), rq_ref, None)
    rope_store(zcol(1), rk_ref, DK_R ** -0.5)
    rv_ref[...] = zcol(2).astype(BF16)
    srg_ref[...] = silu(zcol(3)).astype(BF16)
    fq_ref[...] = head_norm(zcol(4), qg_ref[...]).astype(BF16)
    fk = head_norm(zcol(5), kg_ref[...])
    fk_ref[...] = fk
    fkb_ref[...] = fk.astype(BF16)
    fv = zcol(6)
    fv_ref[...] = fv
    fvb_ref[...] = fv.astype(BF16)
    sfg_ref[...] = silu(zcol(7)).astype(BF16)

    ff = lax.dot_general(wff_ref[...], h, NT_DIMS, preferred_element_type=F32) + bf_ref[...]
    lf = jnp.minimum(ff, 0.0) - jnp.log1p(jnp.exp(-jnp.abs(ff)))
    lf_ref[...] = lf
    c_ref[...] = _seg_cumsum(lf, seg) + carry_ref[...]
    carry_ref[...] = carry_ref[...] + jnp.sum(lf, axis=1, keepdims=True)


def _proj(x2d, ng, w_main, wff_t, bf, cos_t, sin_t, qg, kg, ones_bd, *, tm, seg, tiles_per_seq):
    n, d_model = x2d.shape
    nt = n // tm
    row = lambda i: (i, 0)
    const = lambda i: (0, 0)
    tab = lambda i: (i % tiles_per_seq, 0)
    col = lambda i: (0, i)
    bf16_out = jax.ShapeDtypeStruct((n, D_RET), BF16)
    f32_out = jax.ShapeDtypeStruct((n, D_FOX), F32)
    t_out = jax.ShapeDtypeStruct((H_F, n), F32)
    blk = pl.BlockSpec((tm, D_RET), row)
    tblk = pl.BlockSpec((H_F, tm), col)
    return pl.pallas_call(
        functools.partial(_proj_kernel, seg=seg, tiles_per_seq=tiles_per_seq),
        grid=(nt,),
        in_specs=[
            pl.BlockSpec((tm, d_model), row),
            pl.BlockSpec((1, d_model), const),
            pl.BlockSpec(w_main.shape, const),
            pl.BlockSpec(wff_t.shape, const),
            pl.BlockSpec((H_F, 1), const),
            pl.BlockSpec((tm, LANES), tab),
            pl.BlockSpec((tm, LANES), tab),
            pl.BlockSpec((1, D_FOX), const),
            pl.BlockSpec((1, D_FOX), const),
            pl.BlockSpec((D_FOX, D_FOX), const),
        ],
        out_specs=[blk, blk, blk, blk, blk, blk, blk, blk, blk, blk, tblk, tblk],
        out_shape=[bf16_out, bf16_out, bf16_out, bf16_out, bf16_out, f32_out, bf16_out, f32_out,
                   bf16_out, bf16_out, t_out, t_out],
        scratch_shapes=[pltpu.VMEM((H_F, 1), F32)],
        compiler_params=_cparams(("arbitrary",)),
        name="proj",
    )(x2d, ng, w_main, wff_t, bf, cos_t, sin_t, qg, kg, ones_bd)


def _ret_kernel(q_ref, k_ref, v_ref, g_ref, s0_ref, dec_ref, qd_ref, kd_ref, sd_ref, rg_ref,
                u_ref, sfin_ref, state_ref, *, chunk, n_chunks):
    j = pl.program_id(1)

    @pl.when(j == 0)
    def _():
        state_ref[...] = s0_ref[...]

    for ci in range(n_chunks):
        rows = slice(ci * chunk, (ci + 1) * chunk)
        for hh in range(H_R):
            cols = slice(hh * DK_R, (hh + 1) * DK_R)
            q = q_ref[rows, cols]
            k = k_ref[rows, cols]
            v = v_ref[rows, cols]
            s = state_ref[hh]
            sc = lax.dot_general(q, k, NT_DIMS, preferred_element_type=F32) * dec_ref[hh]
            o = (jnp.dot(sc.astype(BF16), v, preferred_element_type=F32)
                 + jnp.dot(q, s.astype(BF16), preferred_element_type=F32) * qd_ref[hh])
            kdec = (k.astype(F32) * kd_ref[hh]).astype(BF16)
            state_ref[hh] = (sd_ref[hh] * s
                             + lax.dot_general(kdec, v, TN_DIMS, preferred_element_type=F32))
            ms = jnp.mean(o * o, axis=-1, keepdims=True)
            u = o * lax.rsqrt(ms + EPS) * rg_ref[:, cols] * g_ref[rows, cols].astype(F32)
            u_ref[rows, cols] = u.astype(BF16)

    @pl.when(j == pl.num_programs(1) - 1)
    def _():
        sfin_ref[...] = state_ref[...]


def _retention_tables(chunk, c_valid):
    lg = jnp.log(1.0 - 2.0 ** (-5.0 - jnp.arange(H_R, dtype=F32)))
    idx = jnp.arange(chunk, dtype=F32)
    diff = idx[:, None] - idx[None, :]
    dec = jnp.where(diff >= 0, jnp.exp(jnp.maximum(diff, 0.0)[None] * lg[:, None, None]), 0.0)
    qd = jnp.exp((idx + 1.0)[None, :] * lg[:, None])
    kd = jnp.exp((c_valid - 1.0 - idx)[None, :] * lg[:, None])
    sd = jnp.exp(c_valid * lg)
    qd = jnp.broadcast_to(qd[:, :, None], (H_R, chunk, DK_R))
    kd = jnp.broadcast_to(kd[:, :, None], (H_R, chunk, DK_R))
    sd = jnp.broadcast_to(sd[:, None, None], (H_R, 1, DK_R))
    return dec, qd, kd, sd


def _retention(q, k, v, g, s0, ret_g, *, tile, chunk, c_valid):
    b, t, _ = q.shape
    dec, qd, kd, sd = _retention_tables(chunk, c_valid)
    seq = pl.BlockSpec((None, tile, D_RET), lambda bi, j: (bi, j, 0))
    st = pl.BlockSpec((None, H_R, DK_R, DK_R), lambda bi, j: (bi, 0, 0, 0))

    def full(a):
        return pl.BlockSpec(a.shape, lambda bi, j: (0,) * a.ndim)

    return pl.pallas_call(
        functools.partial(_ret_kernel, chunk=chunk, n_chunks=tile // chunk),
        grid=(b, t // tile),
        in_specs=[seq, seq, seq, seq, st, full(dec), full(qd), full(kd), full(sd), full(ret_g)],
        out_specs=[seq, st],
        out_shape=[jax.ShapeDtypeStruct((b, t, D_RET), BF16),
                   jax.ShapeDtypeStruct((b, H_R, DK_R, DK_R), F32)],
        scratch_shapes=[pltpu.VMEM((H_R, DK_R, DK_R), F32)],
        compiler_params=_cparams(("arbitrary", "arbitrary")),
        name="retention",
    )(q, k, v, g, s0, dec, qd, kd, sd, ret_g)


def _attn_kernel(q_ref, k_ref, v_ref, c_ref, g_ref, o_ref, m_ref, acc_ref, *, tq, tk, seq):
    pair = pl.program_id(1)
    nchunk = tk // LANES
    first_q = lax.broadcasted_iota(jnp.int32, (tq, LANES), 1) < DH_F
    first_k = lax.broadcasted_iota(jnp.int32, (tk, LANES), 1) < DH_F
    rowi = lax.broadcasted_iota(jnp.int32, (tq, LANES), 0)
    coli = lax.broadcasted_iota(jnp.int32, (tq, LANES), 1)

    def q_block(i, carry):
        q0 = pl.multiple_of(i * tq, tq)
        q = q_ref[pl.ds(q0, tq), :]
        zero = jnp.zeros_like(q)
        qs = (jnp.where(first_q, q, zero), jnp.where(first_q, zero, q))
        m_ref[...] = jnp.full(m_ref.shape, NEG, F32)
        acc_ref[...] = jnp.zeros_like(acc_ref)

        def kv_block(j, diag_offset):
            k0 = pl.multiple_of(j * tk, tk)
            kb = k_ref[pl.ds(k0, tk), :]
            vb = v_ref[pl.ds(k0, tk), :]
            one = jnp.ones_like(vb)
            vs = (jnp.where(first_k, vb, one), jnp.where(first_k, one, vb))
            for hh in range(2):
                ck = c_ref[pl.ds(2 * pair + hh, 1), pl.ds(k0, tk)] * LOG2E
                s = lax.dot_general(qs[hh], kb, NT_DIMS, preferred_element_type=F32)
                chunks = []
                for c in range(nchunk):
                    ch = s[:, c * LANES:(c + 1) * LANES] - ck[:, c * LANES:(c + 1) * LANES]
                    if diag_offset is not None:
                        ch = jnp.where(rowi >= coli + (diag_offset + c * LANES), ch, NEG)
                    chunks.append(ch)
                mx = chunks[0]
                for c in range(1, nchunk):
                    mx = jnp.maximum(mx, chunks[c])
                m_prev = m_ref[hh]
                m_new = jnp.maximum(m_prev, jnp.max(mx, axis=1, keepdims=True))
                alpha = jnp.exp2(m_prev - m_new)
                p = jnp.concatenate([jnp.exp2(ch - m_new) for ch in chunks], axis=1).astype(BF16)
                acc_ref[hh] = acc_ref[hh] * alpha + jnp.dot(p, vs[hh], preferred_element_type=F32)
                m_ref[hh] = m_new

        r = tq // tk
        n_full = i * r

        def body(j, c2):
            kv_block(j, None)
            return c2

        lax.fori_loop(0, n_full, body, 0)
        for jj in range(r):
            kv_block(n_full + jj, jj * tk)

        acc_a = acc_ref[0]
        acc_b = acc_ref[1]
        num = jnp.where(first_q, acc_a, acc_b)
        den = jnp.where(first_q, pltpu.roll(acc_a, DH_F, axis=1), pltpu.roll(acc_b, DH_F, axis=1))
        o_ref[pl.ds(q0, tq), :] = (num / den * g_ref[pl.ds(q0, tq), :].astype(F32)).astype(BF16)
        return carry

    lax.fori_loop(0, seq // tq, q_block, 0)


def _attention(fq, fkb, fvb, c_t, sfg, *, batch, seq, tq, tk):
    n = batch * seq
    blk = pl.BlockSpec((seq, LANES), lambda b, p: (b, p))
    cblk = pl.BlockSpec((H_F, seq), lambda b, p: (0, b))
    return pl.pallas_call(
        functools.partial(_attn_kernel, tq=tq, tk=tk, seq=seq),
        grid=(batch, H_F // 2),
        in_specs=[blk, blk, blk, cblk, blk],
        out_specs=blk,
        out_shape=jax.ShapeDtypeStruct((n, D_FOX), BF16),
        scratch_shapes=[pltpu.VMEM((2, tq, LANES), F32), pltpu.VMEM((2, tq, LANES), F32)],
        compiler_params=_cparams(("arbitrary", "arbitrary")),
        name="attention",
    )(fq, fkb, fvb, c_t, sfg)


def _bias_kernel(pt_ref, lf_hbm, out_ref, buf, sem, *, n_pages):
    li = pl.program_id(0)
    b = pl.program_id(1)

    def page_copy(p):
        return pltpu.make_async_copy(lf_hbm.at[li, pt_ref[b, p]], buf.at[p], sem)

    def issue(p, carry):
        page_copy(p).start()
        return carry

    def wait(p, carry):
        page_copy(p).wait()
        return carry

    lax.fori_loop(0, n_pages, issue, 0)
    lax.fori_loop(0, n_pages, wait, 0)

    x = buf[...].reshape(n_pages * H_F, PAGE)
    lane = lax.broadcasted_iota(jnp.int32, x.shape, 1)
    s = x
    d = 1
    while d < PAGE:
        s = s + jnp.where(lane + d < PAGE, pltpu.roll(s, PAGE - d, axis=1), 0.0)
        d *= 2
    s3 = s.reshape(n_pages, H_F, PAGE)
    tot = jnp.broadcast_to(s3[:, :, 0:1], s3.shape)
    e = tot
    d = 1
    while d < n_pages:
        e = e + jnp.concatenate([e[d:], jnp.zeros((d,) + e.shape[1:], F32)], axis=0)
        d *= 2
    res = s3 - buf[...] + (e - tot)
    for p in range(n_pages):
        out_ref[:, p * PAGE:(p + 1) * PAGE] = res[p]


def _past_bias(lf_t, page_table):
    depth = lf_t.shape[0]
    db, n_pages = page_table.shape
    grid_spec = pltpu.PrefetchScalarGridSpec(
        num_scalar_prefetch=1,
        grid=(depth, db),
        in_specs=[pl.BlockSpec(memory_space=pl.ANY)],
        out_specs=pl.BlockSpec((None, None, H_F, n_pages * PAGE), lambda li, b, pt: (li, b, 0, 0)),
        scratch_shapes=[pltpu.VMEM((n_pages, H_F, PAGE), F32), pltpu.SemaphoreType.DMA(())],
    )
    return pl.pallas_call(
        functools.partial(_bias_kernel, n_pages=n_pages),
        grid_spec=grid_spec,
        out_shape=jax.ShapeDtypeStruct((depth, db, H_F, n_pages * PAGE), F32),
        compiler_params=_cparams(("arbitrary", "arbitrary")),
        name="past_bias",
    )(page_table, lf_t)


def _decode_kernel(pt_ref, q_ref, kn_ref, vn_ref, bn_ref, bias_ref, g_ref, *rest, npg, t_new):
    k_refs = rest[:npg]
    v_refs = rest[npg:2 * npg]
    o_ref, m_ref, l_ref, acc_ref = rest[2 * npg:]
    c = pl.program_id(1)
    q = q_ref[...]
    rows = q.shape[0]

    @pl.when(c == 0)
    def _():
        s = lax.dot_general(q, kn_ref[...], NT_DIMS, preferred_element_type=F32) + bn_ref[...]
        m = jnp.max(s, axis=1, keepdims=True)
        p = jnp.exp2(s - m)
        m_ref[...] = m
        l_ref[...] = jnp.sum(p, axis=1, keepdims=True)
        acc_ref[...] = jnp.dot(p.astype(BF16), vn_ref[...], preferred_element_type=F32)

    ss = []
    for p in range(npg):
        kt = k_refs[p][...].reshape(D_FOX, PAGE).astype(BF16)
        b8 = bias_ref[:, p * PAGE:(p + 1) * PAGE] * LOG2E
        s = jnp.dot(q, kt, preferred_element_type=F32)
        ss.append(s + jnp.concatenate([b8] * t_new, axis=0))
    m_cur = ss[0]
    for p in range(1, npg):
        m_cur = jnp.maximum(m_cur, ss[p])
    m_prev = m_ref[...]
    m_new = jnp.maximum(m_prev, jnp.max(m_cur, axis=1, keepdims=True))
    alpha = jnp.exp2(m_prev - m_new)
    acc = alpha * acc_ref[...]
    lsum = jnp.zeros((rows, PAGE), F32)
    for p in range(npg):
        pp = jnp.exp2(ss[p] - m_new)
        lsum = lsum + pp
        vt = v_refs[p][...].reshape(D_FOX, PAGE).astype(BF16)
        acc = acc + lax.dot_general(pp.astype(BF16), vt, NT_DIMS, preferred_element_type=F32)
    m_ref[...] = m_new
    l_ref[...] = alpha * l_ref[...] + jnp.sum(lsum, axis=1, keepdims=True)
    acc_ref[...] = acc

    @pl.when(c == pl.num_programs(1) - 1)
    def _():
        o = acc_ref[...] / l_ref[...]
        rowh = lax.broadcasted_iota(jnp.int32, (rows, D_FOX), 0) % H_F
        colh = lax.broadcasted_iota(jnp.int32, (rows, D_FOX), 1) // DH_F
        o = jnp.where(rowh == colh, o, 0.0)
        o4 = jnp.sum(o.reshape(t_new, H_F, D_FOX), axis=1)
        o_ref[...] = o4 * g_ref[...]


def _decode(page_table, qbd, kn, vn, bias_new, bias_past, sfg, cache_kt, cache_vt, *, layer, npg, t_new):
    db, n_pages = page_table.shape
    rows = qbd.shape[1]
    per_b = lambda a: pl.BlockSpec((None,) + a.shape[1:], lambda b, c, pt: (b,) + (0,) * (a.ndim - 1))

    def page_spec(p):
        return pl.BlockSpec((None, None, H_F, DH_F, PAGE),
                            lambda b, c, pt: (layer, pt[b, c * npg + p], 0, 0, 0))

    grid_spec = pltpu.PrefetchScalarGridSpec(
        num_scalar_prefetch=1,
        grid=(db, n_pages // npg),
        in_specs=([per_b(qbd), per_b(kn), per_b(vn), per_b(bias_new),
                   pl.BlockSpec((None, None, H_F, npg * PAGE), lambda b, c, pt: (layer, b, 0, c)),
                   per_b(sfg)]
                  + [page_spec(p) for p in range(npg)] + [page_spec(p) for p in range(npg)]),
        out_specs=pl.BlockSpec((None, t_new, D_FOX), lambda b, c, pt: (b, 0, 0)),
        scratch_shapes=[pltpu.VMEM((rows, 1), F32), pltpu.VMEM((rows, 1), F32),
                        pltpu.VMEM((rows, D_FOX), F32)],
    )
    return pl.pallas_call(
        functools.partial(_decode_kernel, npg=npg, t_new=t_new),
        grid_spec=grid_spec,
        out_shape=jax.ShapeDtypeStruct((db, t_new, D_FOX), F32),
        compiler_params=_cparams(("arbitrary", "arbitrary")),
        name="decode",
    )(page_table, qbd, kn, vn, bias_new, bias_past, sfg, *([cache_kt] * npg), *([cache_vt] * npg))


def _merge_kernel(x_ref, ur_ref, uf_ref, w_ref, y_ref):
    y_ref[...] = (x_ref[...]
                  + jnp.dot(ur_ref[...], w_ref[:D_RET, :], preferred_element_type=F32)
                  + jnp.dot(uf_ref[...], w_ref[D_RET:, :], preferred_element_type=F32))


def _merge(x2d, u_ret, u_fox, w_out, *, tm):
    n, d_model = x2d.shape
    row = lambda i: (i, 0)
    return pl.pallas_call(
        _merge_kernel,
        grid=(n // tm,),
        in_specs=[pl.BlockSpec((tm, d_model), row), pl.BlockSpec((tm, D_RET), row),
                  pl.BlockSpec((tm, D_FOX), row), pl.BlockSpec(w_out.shape, lambda i: (0, 0))],
        out_specs=pl.BlockSpec((tm, d_model), row),
        out_shape=jax.ShapeDtypeStruct((n, d_model), F32),
        compiler_params=_cparams(("parallel",)),
        name="merge",
    )(x2d, u_ret, u_fox, w_out)


def _rope_tables(pos):
    inv = 1.0 / (ROPE_BASE ** (jnp.arange(0, DK_R, 2, dtype=F32) / DK_R))
    ang = pos[:, None] * inv[None, :]
    cos = jnp.cos(ang)
    sin = jnp.sin(ang)
    return jnp.concatenate([cos, cos], axis=-1), jnp.concatenate([-sin, sin], axis=-1)


def kernel(x_prompt, x_sample, cache_fox_k, cache_fox_v, cache_fox_logf, state_ret, page_table,
           norm_gain, w_in, ret_norm_gain, fox_q_gain, fox_k_gain, fox_b_f, w_out):
    batch, seq, d_model = x_prompt.shape
    db, t_new, _ = x_sample.shape
    depth = w_in.shape[0]
    n_pool = cache_fox_k.shape[1]
    n_pages = page_table.shape[1]
    past_len = n_pages * PAGE
    n_p = batch * seq
    n_s = db * t_new
    pad_new = 16
    rows_s = t_new * H_F

    tm_p = min(512, n_p)
    tq = min(512, seq)
    tk = min(512, seq)
    ret_tile = min(512, seq)
    npg = min(8, n_pages)

    w_main = w_in[:, :, :4 * D_RET + 4 * D_FOX].astype(BF16)
    wff_t = jnp.swapaxes(w_in[:, :, 4 * D_RET + 4 * D_FOX:], 1, 2).astype(BF16)
    w_out_b = w_out.astype(BF16)
    head_id = jnp.arange(D_FOX, dtype=jnp.int32) // DH_F
    ones_bd = (head_id[:, None] == head_id[None, :]).astype(BF16)
    cos_p, sin_p = _rope_tables(jnp.arange(seq, dtype=F32))
    pos_s = past_len + jnp.arange(t_new, dtype=F32)
    cos_s, sin_s = _rope_tables(jnp.tile(pos_s, db))
    qg = jnp.tile(fox_q_gain, (1, H_F)) * (DH_F ** -0.5 * LOG2E)
    kg = jnp.tile(fox_k_gain, (1, H_F))
    ret_g = ret_norm_gain.reshape(depth, 1, D_RET)

    cache_kt = jnp.transpose(cache_fox_k, (0, 1, 3, 4, 2))
    cache_vt = jnp.transpose(cache_fox_v, (0, 1, 3, 4, 2))
    lf_t = jnp.swapaxes(cache_fox_logf, 2, 3)
    bias_past = _past_bias(lf_t, page_table)

    eye_h = jnp.eye(H_F, dtype=BF16)
    jj = jnp.arange(pad_new)
    new_valid = (jj[None, :] <= jnp.arange(t_new)[:, None]) & (jj[None, :] < t_new)
    s0_p = jnp.zeros((batch, H_R, DK_R, DK_R), F32)

    def pad_rows(a):
        a = a.reshape(db, t_new, a.shape[-1])
        return jnp.pad(a, ((0, 0), (0, pad_new - t_new), (0, 0)))

    xp = x_prompt.reshape(n_p, d_model)
    xs = x_sample.reshape(n_s, d_model)
    outs = [[] for _ in range(8)]
    for l in range(depth):
        proj_args = (norm_gain[l][None, :], w_main[l], wff_t[l], fox_b_f[l][:, None])
        gains = (qg[l][None, :], kg[l][None, :], ones_bd)

        (rq, rk, rv, srg, fq, fk, fkb, fv, fvb, sfg, lf, c_t) = _proj(
            xp, *proj_args, cos_p, sin_p, *gains, tm=tm_p, seg=tm_p, tiles_per_seq=seq // tm_p)
        r3 = lambda a: a.reshape(batch, seq, D_RET)
        u_ret, s_p = _retention(r3(rq), r3(rk), r3(rv), r3(srg), s0_p, ret_g[l],
                                tile=ret_tile, chunk=min(RET_CHUNK, seq), c_valid=float(min(RET_CHUNK, seq)))
        u_fox = _attention(fq, fkb, fvb, c_t, sfg, batch=batch, seq=seq, tq=tq, tk=tk)
        xp = _merge(xp, u_ret.reshape(n_p, D_RET), u_fox, w_out_b[l], tm=tm_p)
        outs[0].append(fk.reshape(batch, seq, H_F, DH_F))
        outs[1].append(fv.reshape(batch, seq, H_F, DH_F))
        outs[2].append(lf.T.reshape(batch, seq, H_F))
        outs[3].append(s_p)

        (rq, rk, rv, srg, fq, fk, fkb, fv, fvb, sfg, lf, c_t) = _proj(
            xs, *proj_args, cos_s, sin_s, *gains, tm=n_s, seg=t_new, tiles_per_seq=1)
        u_ret, s_s = _retention(pad_rows(rq), pad_rows(rk), pad_rows(rv), pad_rows(srg), state_ret[l], ret_g[l],
                                tile=pad_new, chunk=pad_new, c_valid=float(t_new))
        u_ret = u_ret[:, :t_new].reshape(n_s, D_RET)
        q5 = fq.reshape(db, t_new, 1, H_F, DH_F) * eye_h[None, None, :, :, None]
        qbd = q5.reshape(db, rows_s, D_FOX)
        cn = c_t.reshape(H_F, db, t_new).transpose(1, 0, 2)
        bn = jnp.pad(-cn * LOG2E, ((0, 0), (0, 0), (0, pad_new - t_new)))
        bn = jnp.where(new_valid[None, :, None, :], bn[:, None, :, :], NEG)
        bn = bn.reshape(db, rows_s, pad_new)
        u_fox = _decode(page_table, qbd, pad_rows(fkb), pad_rows(fvb), bn, bias_past,
                        sfg.astype(F32).reshape(db, t_new, D_FOX), cache_kt, cache_vt,
                        layer=l, npg=npg, t_new=t_new)
        xs = _merge(xs, u_ret, u_fox.reshape(n_s, D_FOX).astype(BF16), w_out_b[l], tm=n_s)
        outs[4].append(fk.reshape(db, t_new, H_F, DH_F))
        outs[5].append(fv.reshape(db, t_new, H_F, DH_F))
        outs[6].append(lf.T.reshape(db, t_new, H_F))
        outs[7].append(s_s)

    return (xp.reshape(batch, seq, d_model), xs.reshape(db, t_new, d_model),
            jnp.stack(outs[0]), jnp.stack(outs[1]), jnp.stack(outs[2]), jnp.stack(outs[3]),
            jnp.stack(outs[4]), jnp.stack(outs[5]), jnp.stack(outs[6]), jnp.stack(outs[7]))
```

```python
import functools
import math

import jax
import jax.numpy as jnp
from jax import lax
from jax.experimental import pallas as pl
from jax.experimental.pallas import tpu as pltpu

F32 = jnp.float32
BF16 = jnp.bfloat16

H_R = 4
DK_R = 128
H_F = 8
DH_F = 64
D_RET = H_R * DK_R
D_FOX = H_F * DH_F
PAGE = 128
RET_CHUNK = 128
ROPE_BASE = 10000.0
EPS = 1e-6
NEG = -1e30
LOG2E = math.log2(math.e)

LANES = 128
SUBLANES = 8
VMEM_LIMIT = 56 * 1024 * 1024

NT_DIMS = (((1,), (1,)), ((), ()))
TN_DIMS = (((0,), (0,)), ((), ()))


def _cparams(sem):
    return pltpu.CompilerParams(dimension_semantics=sem, vmem_limit_bytes=VMEM_LIMIT)


def _seg_cumsum(x, seg):
    lane = lax.broadcasted_iota(jnp.int32, x.shape, 1)
    pos = lane & (seg - 1)
    d = 1
    while d < seg:
        shifted = pltpu.roll(x, d, axis=1)
        x = x + jnp.where(pos >= d, shifted, 0.0)
        d *= 2
    return x


def _proj_kernel(x_ref, ng_ref, w_ref, wkt_ref, wvt_ref, wff_ref, bf_ref, cos_ref, sin_ref, qg_ref, kgt_ref,
                 ones_ref, kstack_ref, vstack_ref,
                 rq_ref, rk_ref, rv_ref, srg_ref, fq_ref, fkt_ref, fktb_ref, fvt_ref, fvb_ref, sfg_ref,
                 lf_ref, c_ref, carry_ref, *, seg, tiles_per_seq):
    del kstack_ref, vstack_ref
    i = pl.program_id(0)

    @pl.when(i % tiles_per_seq == 0)
    def _():
        carry_ref[...] = jnp.zeros_like(carry_ref)

    x = x_ref[...]
    ms = jnp.mean(x * x, axis=-1, keepdims=True)
    h = (x * lax.rsqrt(ms + EPS) * ng_ref[...]).astype(BF16)

    def zcol(g):
        return jnp.dot(h, w_ref[:, g * D_RET:(g + 1) * D_RET], preferred_element_type=F32)

    cos = cos_ref[...]
    sin = sin_ref[...]

    def rope_store(z, out_ref, scale):
        for hh in range(H_R):
            zz = z[:, hh * DK_R:(hh + 1) * DK_R]
            rot = pltpu.roll(zz, DK_R // 2, axis=1)
            r = zz * cos + rot * sin
            if scale is not None:
                r = r * scale
            out_ref[:, hh * DK_R:(hh + 1) * DK_R] = r.astype(out_ref.dtype)

    def silu(z):
        return z / (1.0 + jnp.exp(-z))

    def head_norm(z, gain):
        y = z * z
        yh = y.astype(BF16)
        yl = (y - yh.astype(F32)).astype(BF16)
        ssq = (jnp.dot(yh, ones_ref[...], preferred_element_type=F32)
               + jnp.dot(yl, ones_ref[...], preferred_element_type=F32))
        return z * lax.rsqrt(ssq * (1.0 / DH_F) + EPS) * gain

    rope_store(zcol(0), rq_ref, None)
    rope_store(zcol(1), rk_ref, DK_R ** -0.5)
    rv_ref[...] = zcol(2).astype(BF16)
    srg_ref[...] = silu(zcol(3)).astype(BF16)
    fq_ref[...] = head_norm(zcol(4), qg_ref[...]).astype(BF16)
    zk = lax.dot_general(wkt_ref[...], h, NT_DIMS, preferred_element_type=F32)
    y = zk * zk
    yh = y.astype(BF16)
    yl = (y - yh.astype(F32)).astype(BF16)
    ssq = (jnp.dot(ones_ref[...], yh, preferred_element_type=F32)
           + jnp.dot(ones_ref[...], yl, preferred_element_type=F32))
    kgt = jnp.concatenate([kgt_ref[...]] * (zk.shape[1] // LANES), axis=1)
    fkt = zk * lax.rsqrt(ssq * (1.0 / DH_F) + EPS) * kgt
    fkt_ref[...] = fkt
    fktb_ref[...] = fkt.astype(BF16)
    fvt_ref[...] = lax.dot_general(wvt_ref[...], h, NT_DIMS, preferred_element_type=F32)
    fvb_ref[...] = zcol(6).astype(BF16)
    sfg_ref[...] = silu(zcol(7)).astype(BF16)

    ff = lax.dot_general(wff_ref[...], h, NT_DIMS, preferred_element_type=F32) + bf_ref[...]
    lf = jnp.minimum(ff, 0.0) - jnp.log1p(jnp.exp(-jnp.abs(ff)))
    lf_ref[...] = lf
    c_ref[...] = _seg_cumsum(lf, seg) + carry_ref[...]
    carry_ref[...] = carry_ref[...] + jnp.sum(lf, axis=1, keepdims=True)


def _proj(x2d, ng, w_main, wk_t, wv_t, wff_t, bf, cos_t, sin_t, qg, kg_t, ones_bd, kstack, vstack,
          *, layer, tm, seg, tiles_per_seq):
    n, d_model = x2d.shape
    nt = n // tm
    n_seq = kstack.shape[1]
    seq_len = kstack.shape[3]
    row = lambda i: (i, 0)
    const = lambda i: (0, 0)
    tab = lambda i: (i % tiles_per_seq, 0)
    col = lambda i: (0, i)
    bf16_out = jax.ShapeDtypeStruct((n, D_RET), BF16)
    t_out = jax.ShapeDtypeStruct((H_F, n), F32)
    blk = pl.BlockSpec((tm, D_RET), row)
    tblk = pl.BlockSpec((H_F, tm), col)
    stack_blk = pl.BlockSpec((None, None, D_FOX, tm),
                             lambda i: (layer, i // tiles_per_seq, 0, i % tiles_per_seq))
    ktb_blk = pl.BlockSpec((None, D_FOX, tm), lambda i: (i // tiles_per_seq, 0, i % tiles_per_seq))
    stack_out = jax.ShapeDtypeStruct(kstack.shape, F32)
    any_spec = pl.BlockSpec(memory_space=pl.ANY)
    return pl.pallas_call(
        functools.partial(_proj_kernel, seg=seg, tiles_per_seq=tiles_per_seq),
        grid=(nt,),
        in_specs=[
            pl.BlockSpec((tm, d_model), row),
            pl.BlockSpec((1, d_model), const),
            pl.BlockSpec(w_main.shape, const),
            pl.BlockSpec(wk_t.shape, const),
            pl.BlockSpec(wv_t.shape, const),
            pl.BlockSpec(wff_t.shape, const),
            pl.BlockSpec((H_F, 1), const),
            pl.BlockSpec((tm, LANES), tab),
            pl.BlockSpec((tm, LANES), tab),
            pl.BlockSpec((1, D_FOX), const),
            pl.BlockSpec((D_FOX, LANES), const),
            pl.BlockSpec((D_FOX, D_FOX), const),
            any_spec,
            any_spec,
        ],
        out_specs=[blk, blk, blk, blk, blk, stack_blk, ktb_blk, stack_blk, blk, blk, tblk, tblk],
        out_shape=[bf16_out, bf16_out, bf16_out, bf16_out, bf16_out, stack_out,
                   jax.ShapeDtypeStruct((n_seq, D_FOX, seq_len), BF16), stack_out,
                   bf16_out, bf16_out, t_out, t_out],
        input_output_aliases={12: 5, 13: 7},
        scratch_shapes=[pltpu.VMEM((H_F, 1), F32)],
        compiler_params=_cparams(("arbitrary",)),
        name="proj",
    )(x2d, ng, w_main, wk_t, wv_t, wff_t, bf, cos_t, sin_t, qg, kg_t, ones_bd, kstack, vstack)


def _ret_kernel(q_ref, k_ref, v_ref, g_ref, s0_ref, dec_ref, qd_ref, kd_ref, sd_ref, rg_ref,
                u_ref, sfin_ref, state_ref, *, chunk, n_chunks):
    j = pl.program_id(1)

    @pl.when(j == 0)
    def _():
        state_ref[...] = s0_ref[...]

    for ci in range(n_chunks):
        rows = slice(ci * chunk, (ci + 1) * chunk)
        for hh in range(H_R):
            cols = slice(hh * DK_R, (hh + 1) * DK_R)
            q = q_ref[rows, cols]
            k = k_ref[rows, cols]
            v = v_ref[rows, cols]
            s = state_ref[hh]
            sc = lax.dot_general(q, k, NT_DIMS, preferred_element_type=F32) * dec_ref[hh]
            o = (jnp.dot(sc.astype(BF16), v, preferred_element_type=F32)
                 + jnp.dot(q, s.astype(BF16), preferred_element_type=F32) * qd_ref[hh])
            kdec = (k.astype(F32) * kd_ref[hh]).astype(BF16)
            state_ref[hh] = (sd_ref[hh] * s
                             + lax.dot_general(kdec, v, TN_DIMS, preferred_element_type=F32))
            ms = jnp.mean(o * o, axis=-1, keepdims=True)
            u = o * lax.rsqrt(ms + EPS) * rg_ref[:, cols] * g_ref[rows, cols].astype(F32)
            u_ref[rows, cols] = u.astype(BF16)

    @pl.when(j == pl.num_programs(1) - 1)
    def _():
        sfin_ref[...] = state_ref[...]


def _retention_tables(chunk, c_valid):
    lg = jnp.log(1.0 - 2.0 ** (-5.0 - jnp.arange(H_R, dtype=F32)))
    idx = jnp.arange(chunk, dtype=F32)
    diff = idx[:, None] - idx[None, :]
    dec = jnp.where(diff >= 0, jnp.exp(jnp.maximum(diff, 0.0)[None] * lg[:, None, None]), 0.0)
    qd = jnp.exp((idx + 1.0)[None, :] * lg[:, None])
    kd = jnp.exp((c_valid - 1.0 - idx)[None, :] * lg[:, None])
    sd = jnp.exp(c_valid * lg)
    qd = jnp.broadcast_to(qd[:, :, None], (H_R, chunk, DK_R))
    kd = jnp.broadcast_to(kd[:, :, None], (H_R, chunk, DK_R))
    sd = jnp.broadcast_to(sd[:, None, None], (H_R, 1, DK_R))
    return dec, qd, kd, sd


def _retention(q, k, v, g, s0, ret_g, *, tile, chunk, c_valid):
    b, t, _ = q.shape
    dec, qd, kd, sd = _retention_tables(chunk, c_valid)
    seq = pl.BlockSpec((None, tile, D_RET), lambda bi, j: (bi, j, 0))
    st = pl.BlockSpec((None, H_R, DK_R, DK_R), lambda bi, j: (bi, 0, 0, 0))

    def full(a):
        return pl.BlockSpec(a.shape, lambda bi, j: (0,) * a.ndim)

    return pl.pallas_call(
        functools.partial(_ret_kernel, chunk=chunk, n_chunks=tile // chunk),
        grid=(b, t // tile),
        in_specs=[seq, seq, seq, seq, st, full(dec), full(qd), full(kd), full(sd), full(ret_g)],
        out_specs=[seq, st],
        out_shape=[jax.ShapeDtypeStruct((b, t, D_RET), BF16),
                   jax.ShapeDtypeStruct((b, H_R, DK_R, DK_R), F32)],
        scratch_shapes=[pltpu.VMEM((H_R, DK_R, DK_R), F32)],
        compiler_params=_cparams(("arbitrary", "arbitrary")),
        name="retention",
    )(q, k, v, g, s0, dec, qd, kd, sd, ret_g)


def _attn_kernel(q_ref, kt_ref, v_ref, c_ref, g_ref, o_ref, m_ref, acc_ref, *, tq, tk, seq):
    pair = pl.program_id(1)
    nchunk = tk // LANES
    first_q = lax.broadcasted_iota(jnp.int32, (tq, LANES), 1) < DH_F
    first_k = lax.broadcasted_iota(jnp.int32, (tk, LANES), 1) < DH_F

    def q_block(i, carry):
        q0 = pl.multiple_of(i * tq, tq)
        q = q_ref[pl.ds(q0, tq), :]
        zero = jnp.zeros_like(q)
        qs = (jnp.where(first_q, q, zero), jnp.where(first_q, zero, q))
        m_ref[...] = jnp.full(m_ref.shape, NEG, F32)
        acc_ref[...] = jnp.zeros_like(acc_ref)

        def kv_block(j, row0, masked):
            k0 = pl.multiple_of(j * tk, tk)
            kt = kt_ref[:, pl.ds(k0, tk)]
            vb = v_ref[pl.ds(k0, tk), :]
            one = jnp.ones_like(vb)
            vs = (jnp.where(first_k, vb, one), jnp.where(first_k, one, vb))
            for hh in range(2):
                ck = c_ref[pl.ds(2 * pair + hh, 1), pl.ds(k0, tk)] * LOG2E
                s = jnp.dot(qs[hh][row0:], kt, preferred_element_type=F32)
                chunks = []
                for c in range(nchunk):
                    ch = s[:, c * LANES:(c + 1) * LANES] - ck[:, c * LANES:(c + 1) * LANES]
                    if masked:
                        rowi = lax.broadcasted_iota(jnp.int32, ch.shape, 0)
                        coli = lax.broadcasted_iota(jnp.int32, ch.shape, 1)
                        ch = jnp.where(rowi >= coli + c * LANES, ch, NEG)
                    chunks.append(ch)
                mx = chunks[0]
                for c in range(1, nchunk):
                    mx = jnp.maximum(mx, chunks[c])
                m_prev = m_ref[hh, row0:, :]
                m_new = jnp.maximum(m_prev, jnp.max(mx, axis=1, keepdims=True))
                alpha = jnp.exp2(m_prev - m_new)
                p = jnp.concatenate([jnp.exp2(ch - m_new) for ch in chunks], axis=1).astype(BF16)
                acc_ref[hh, row0:, :] = (acc_ref[hh, row0:, :] * alpha
                                         + jnp.dot(p, vs[hh], preferred_element_type=F32))
                m_ref[hh, row0:, :] = m_new

        r = tq // tk
        n_full = i * r

        def body(j, c2):
            kv_block(j, 0, False)
            return c2

        lax.fori_loop(0, n_full, body, 0)
        for jj in range(r):
            kv_block(n_full + jj, jj * tk, True)

        acc_a = acc_ref[0]
        acc_b = acc_ref[1]
        num = jnp.where(first_q, acc_a, acc_b)
        den = jnp.where(first_q, pltpu.roll(acc_a, DH_F, axis=1), pltpu.roll(acc_b, DH_F, axis=1))
        o_ref[pl.ds(q0, tq), :] = (num / den * g_ref[pl.ds(q0, tq), :].astype(F32)).astype(BF16)
        return carry

    lax.fori_loop(0, seq // tq, q_block, 0)


def _attention(fq, fkt, fvb, c_t, sfg, *, batch, seq, tq, tk):
    n = batch * seq
    blk = pl.BlockSpec((seq, LANES), lambda b, p: (b, p))
    ktblk = pl.BlockSpec((None, LANES, seq), lambda b, p: (b, p, 0))
    cblk = pl.BlockSpec((H_F, seq), lambda b, p: (0, b))
    return pl.pallas_call(
        functools.partial(_attn_kernel, tq=tq, tk=tk, seq=seq),
        grid=(batch, H_F // 2),
        in_specs=[blk, ktblk, blk, cblk, blk],
        out_specs=blk,
        out_shape=jax.ShapeDtypeStruct((n, D_FOX), BF16),
        scratch_shapes=[pltpu.VMEM((2, tq, LANES), F32), pltpu.VMEM((2, tq, LANES), F32)],
        compiler_params=_cparams(("arbitrary", "arbitrary")),
        name="attention",
    )(fq, fkt, fvb, c_t, sfg)


def _bias_kernel(pt_ref, lf_hbm, out_ref, buf, sem, *, n_pages):
    li = pl.program_id(0)
    b = pl.program_id(1)

    def page_copy(p):
        return pltpu.make_async_copy(lf_hbm.at[li, pt_ref[b, p]], buf.at[p], sem)

    def issue(p, carry):
        page_copy(p).start()
        return carry

    def wait(p, carry):
        page_copy(p).wait()
        return carry

    lax.fori_loop(0, n_pages, issue, 0)
    lax.fori_loop(0, n_pages, wait, 0)

    x = buf[...].reshape(n_pages * H_F, PAGE)
    lane = lax.broadcasted_iota(jnp.int32, x.shape, 1)
    s = x
    d = 1
    while d < PAGE:
        s = s + jnp.where(lane + d < PAGE, pltpu.roll(s, PAGE - d, axis=1), 0.0)
        d *= 2
    s3 = s.reshape(n_pages, H_F, PAGE)
    tot = jnp.broadcast_to(s3[:, :, 0:1], s3.shape)
    e = tot
    d = 1
    while d < n_pages:
        e = e + jnp.concatenate([e[d:], jnp.zeros((d,) + e.shape[1:], F32)], axis=0)
        d *= 2
    res = s3 - buf[...] + (e - tot)
    for p in range(n_pages):
        out_ref[:, p * PAGE:(p + 1) * PAGE] = res[p]


def _past_bias(lf_t, page_table):
    depth = lf_t.shape[0]
    db, n_pages = page_table.shape
    grid_spec = pltpu.PrefetchScalarGridSpec(
        num_scalar_prefetch=1,
        grid=(depth, db),
        in_specs=[pl.BlockSpec(memory_space=pl.ANY)],
        out_specs=pl.BlockSpec((None, None, H_F, n_pages * PAGE), lambda li, b, pt: (li, b, 0, 0)),
        scratch_shapes=[pltpu.VMEM((n_pages, H_F, PAGE), F32), pltpu.SemaphoreType.DMA(())],
    )
    return pl.pallas_call(
        functools.partial(_bias_kernel, n_pages=n_pages),
        grid_spec=grid_spec,
        out_shape=jax.ShapeDtypeStruct((depth, db, H_F, n_pages * PAGE), F32),
        compiler_params=_cparams(("arbitrary", "arbitrary")),
        name="past_bias",
    )(page_table, lf_t)


def _decode_kernel(pt_ref, q_ref, kn_ref, vn_ref, bn_ref, bias_ref, g_ref, *rest, npg, t_new):
    k_refs = rest[:npg]
    v_refs = rest[npg:2 * npg]
    o_ref, m_ref, l_ref, acc_ref = rest[2 * npg:]
    c = pl.program_id(1)
    q = q_ref[...]
    rows = q.shape[0]

    @pl.when(c == 0)
    def _():
        s = lax.dot_general(q, kn_ref[...], NT_DIMS, preferred_element_type=F32) + bn_ref[...]
        m = jnp.max(s, axis=1, keepdims=True)
        p = jnp.exp2(s - m)
        m_ref[...] = m
        l_ref[...] = jnp.sum(p, axis=1, keepdims=True)
        acc_ref[...] = jnp.dot(p.astype(BF16), vn_ref[...], preferred_element_type=F32)

    ss = []
    for p in range(npg):
        kt = k_refs[p][...].reshape(D_FOX, PAGE).astype(BF16)
        b8 = bias_ref[:, p * PAGE:(p + 1) * PAGE] * LOG2E
        s = jnp.dot(q, kt, preferred_element_type=F32)
        ss.append(s + jnp.concatenate([b8] * t_new, axis=0))
    m_cur = ss[0]
    for p in range(1, npg):
        m_cur = jnp.maximum(m_cur, ss[p])
    m_prev = m_ref[...]
    m_new = jnp.maximum(m_prev, jnp.max(m_cur, axis=1, keepdims=True))
    alpha = jnp.exp2(m_prev - m_new)
    acc = alpha * acc_ref[...]
    lsum = jnp.zeros((rows, PAGE), F32)
    for p in range(npg):
        pp = jnp.exp2(ss[p] - m_new)
        lsum = lsum + pp
        vt = v_refs[p][...].reshape(D_FOX, PAGE).astype(BF16)
        acc = acc + lax.dot_general(pp.astype(BF16), vt, NT_DIMS, preferred_element_type=F32)
    m_ref[...] = m_new
    l_ref[...] = alpha * l_ref[...] + jnp.sum(lsum, axis=1, keepdims=True)
    acc_ref[...] = acc

    @pl.when(c == pl.num_programs(1) - 1)
    def _():
        o = acc_ref[...] / l_ref[...]
        rowh = lax.broadcasted_iota(jnp.int32, (rows, D_FOX), 0) % H_F
        colh = lax.broadcasted_iota(jnp.int32, (rows, D_FOX), 1) // DH_F
        o = jnp.where(rowh == colh, o, 0.0)
        o4 = jnp.sum(o.reshape(t_new, H_F, D_FOX), axis=1)
        o_ref[...] = o4 * g_ref[...]


def _decode(page_table, qbd, kn, vn, bias_new, bias_past, sfg, cache_kt, cache_vt, *, layer, npg, t_new):
    db, n_pages = page_table.shape
    rows = qbd.shape[1]
    per_b = lambda a: pl.BlockSpec((None,) + a.shape[1:], lambda b, c, pt: (b,) + (0,) * (a.ndim - 1))

    def page_spec(p):
        return pl.BlockSpec((None, None, H_F, DH_F, PAGE),
                            lambda b, c, pt: (layer, pt[b, c * npg + p], 0, 0, 0))

    grid_spec = pltpu.PrefetchScalarGridSpec(
        num_scalar_prefetch=1,
        grid=(db, n_pages // npg),
        in_specs=([per_b(qbd), per_b(kn), per_b(vn), per_b(bias_new),
                   pl.BlockSpec((None, None, H_F, npg * PAGE), lambda b, c, pt: (layer, b, 0, c)),
                   per_b(sfg)]
                  + [page_spec(p) for p in range(npg)] + [page_spec(p) for p in range(npg)]),
        out_specs=pl.BlockSpec((None, t_new, D_FOX), lambda b, c, pt: (b, 0, 0)),
        scratch_shapes=[pltpu.VMEM((rows, 1), F32), pltpu.VMEM((rows, 1), F32),
                        pltpu.VMEM((rows, D_FOX), F32)],
    )
    return pl.pallas_call(
        functools.partial(_decode_kernel, npg=npg, t_new=t_new),
        grid_spec=grid_spec,
        out_shape=jax.ShapeDtypeStruct((db, t_new, D_FOX), F32),
        compiler_params=_cparams(("arbitrary", "arbitrary")),
        name="decode",
    )(page_table, qbd, kn, vn, bias_new, bias_past, sfg, *([cache_kt] * npg), *([cache_vt] * npg))


def _merge_kernel(x_ref, ur_ref, uf_ref, w_ref, y_ref):
    y_ref[...] = (x_ref[...]
                  + jnp.dot(ur_ref[...], w_ref[:D_RET, :], preferred_element_type=F32)
                  + jnp.dot(uf_ref[...], w_ref[D_RET:, :], preferred_element_type=F32))


def _merge(x2d, u_ret, u_fox, w_out, *, tm):
    n, d_model = x2d.shape
    row = lambda i: (i, 0)
    return pl.pallas_call(
        _merge_kernel,
        grid=(n // tm,),
        in_specs=[pl.BlockSpec((tm, d_model), row), pl.BlockSpec((tm, D_RET), row),
                  pl.BlockSpec((tm, D_FOX), row), pl.BlockSpec(w_out.shape, lambda i: (0, 0))],
        out_specs=pl.BlockSpec((tm, d_model), row),
        out_shape=jax.ShapeDtypeStruct((n, d_model), F32),
        compiler_params=_cparams(("parallel",)),
        name="merge",
    )(x2d, u_ret, u_fox, w_out)


def _rope_tables(pos):
    inv = 1.0 / (ROPE_BASE ** (jnp.arange(0, DK_R, 2, dtype=F32) / DK_R))
    ang = pos[:, None] * inv[None, :]
    cos = jnp.cos(ang)
    sin = jnp.sin(ang)
    return jnp.concatenate([cos, cos], axis=-1), jnp.concatenate([-sin, sin], axis=-1)


def kernel(x_prompt, x_sample, cache_fox_k, cache_fox_v, cache_fox_logf, state_ret, page_table,
           norm_gain, w_in, ret_norm_gain, fox_q_gain, fox_k_gain, fox_b_f, w_out):
    batch, seq, d_model = x_prompt.shape
    db, t_new, _ = x_sample.shape
    depth = w_in.shape[0]
    n_pool = cache_fox_k.shape[1]
    n_pages = page_table.shape[1]
    past_len = n_pages * PAGE
    n_p = batch * seq
    n_s = db * t_new
    pad_new = 16
    rows_s = t_new * H_F

    tm_p = min(512, n_p)
    tq = min(1024, seq)
    tk = min(512, seq)
    ret_tile = min(512, seq)
    npg = min(32, n_pages)

    w_main = w_in[:, :, :4 * D_RET + 4 * D_FOX].astype(BF16)
    k_col = 4 * D_RET + D_FOX
    wk_t = jnp.swapaxes(w_in[:, :, k_col:k_col + D_FOX], 1, 2).astype(BF16)
    wv_t = jnp.swapaxes(w_in[:, :, k_col + D_FOX:k_col + 2 * D_FOX], 1, 2).astype(BF16)
    wff_t = jnp.swapaxes(w_in[:, :, 4 * D_RET + 4 * D_FOX:], 1, 2).astype(BF16)
    w_out_b = w_out.astype(BF16)
    head_id = jnp.arange(D_FOX, dtype=jnp.int32) // DH_F
    ones_bd = (head_id[:, None] == head_id[None, :]).astype(BF16)
    cos_p, sin_p = _rope_tables(jnp.arange(seq, dtype=F32))
    pos_s = past_len + jnp.arange(t_new, dtype=F32)
    cos_s, sin_s = _rope_tables(jnp.tile(pos_s, db))
    qg = jnp.tile(fox_q_gain, (1, H_F)) * (DH_F ** -0.5 * LOG2E)
    kg_t = jnp.broadcast_to(jnp.tile(fox_k_gain, (1, H_F))[:, :, None], (depth, D_FOX, LANES))
    ret_g = ret_norm_gain.reshape(depth, 1, D_RET)
    kstack_p = jnp.zeros((depth, batch, D_FOX, seq), F32)
    vstack_p = jnp.zeros((depth, batch, D_FOX, seq), F32)

    cache_kt = jnp.transpose(cache_fox_k, (0, 1, 3, 4, 2))
    cache_vt = jnp.transpose(cache_fox_v, (0, 1, 3, 4, 2))
    lf_t = jnp.swapaxes(cache_fox_logf, 2, 3)
    bias_past = _past_bias(lf_t, page_table)

    eye_h = jnp.eye(H_F, dtype=BF16)
    jj = jnp.arange(pad_new)
    new_valid = (jj[None, :] <= jnp.arange(t_new)[:, None]) & (jj[None, :] < t_new)
    s0_p = jnp.zeros((batch, H_R, DK_R, DK_R), F32)

    def pad_rows(a):
        a = a.reshape(db, t_new, a.shape[-1])
        return jnp.pad(a, ((0, 0), (0, pad_new - t_new), (0, 0)))

    xp = x_prompt.reshape(n_p, d_model)
    xs = x_sample.reshape(n_s, d_model)
    outs = [[] for _ in range(8)]
    for l in range(depth):
        proj_args = (norm_gain[l][None, :], w_main[l], wk_t[l], wv_t[l], wff_t[l], fox_b_f[l][:, None])
        gains = (qg[l][None, :], kg_t[l], ones_bd)

        (rq, rk, rv, srg, fq, kstack_p, fkt, vstack_p, fvb, sfg, lf, c_t) = _proj(
            xp, *proj_args, cos_p, sin_p, *gains, kstack_p, vstack_p,
            layer=l, tm=tm_p, seg=tm_p, tiles_per_seq=seq // tm_p)
        r3 = lambda a: a.reshape(batch, seq, D_RET)
        u_ret, s_p = _retention(r3(rq), r3(rk), r3(rv), r3(srg), s0_p, ret_g[l],
                                tile=ret_tile, chunk=min(RET_CHUNK, seq), c_valid=float(min(RET_CHUNK, seq)))
        u_fox = _attention(fq, fkt, fvb, c_t, sfg, batch=batch, seq=seq, tq=tq, tk=tk)
        xp = _merge(xp, u_ret.reshape(n_p, D_RET), u_fox, w_out_b[l], tm=tm_p)
        outs[2].append(lf.T.reshape(batch, seq, H_F))
        outs[3].append(s_p)

        one_slab = jnp.zeros((1, 1, D_FOX, n_s), F32)
        (rq, rk, rv, srg, fq, fkt_s, fktb_s, fvt_s, fvb, sfg, lf, c_t) = _proj(
            xs, *proj_args, cos_s, sin_s, *gains, one_slab, one_slab,
            layer=0, tm=n_s, seg=t_new, tiles_per_seq=1)
        fk = fkt_s[0, 0].T
        fv = fvt_s[0, 0].T
        fkb = fktb_s[0].T
        u_ret, s_s = _retention(pad_rows(rq), pad_rows(rk), pad_rows(rv), pad_rows(srg), state_ret[l], ret_g[l],
                                tile=pad_new, chunk=pad_new, c_valid=float(t_new))
        u_ret = u_ret[:, :t_new].reshape(n_s, D_RET)
        q5 = fq.reshape(db, t_new, 1, H_F, DH_F) * eye_h[None, None, :, :, None]
        qbd = q5.reshape(db, rows_s, D_FOX)
        cn = c_t.reshape(H_F, db, t_new).transpose(1, 0, 2)
        bn = jnp.pad(-cn * LOG2E, ((0, 0), (0, 0), (0, pad_new - t_new)))
        bn = jnp.where(new_valid[None, :, None, :], bn[:, None, :, :], NEG)
        bn = bn.reshape(db, rows_s, pad_new)
        u_fox = _decode(page_table, qbd, pad_rows(fkb), pad_rows(fvb), bn, bias_past,
                        sfg.astype(F32).reshape(db, t_new, D_FOX), cache_kt, cache_vt,
                        layer=l, npg=npg, t_new=t_new)
        xs = _merge(xs, u_ret, u_fox.reshape(n_s, D_FOX).astype(BF16), w_out_b[l], tm=n_s)
        outs[4].append(fk.reshape(db, t_new, H_F, DH_F))
        outs[5].append(fv.reshape(db, t_new, H_F, DH_F))
        outs[6].append(lf.T.reshape(db, t_new, H_F))
        outs[7].append(s_s)

    def untranspose(stack):
        return jnp.transpose(stack.reshape(depth, batch, H_F, DH_F, seq), (0, 1, 4, 2, 3))

    return (xp.reshape(batch, seq, d_model), xs.reshape(db, t_new, d_model),
            untranspose(kstack_p), untranspose(vstack_p), jnp.stack(outs[2]), jnp.stack(outs[3]),
            jnp.stack(outs[4]), jnp.stack(outs[5]), jnp.stack(outs[6]), jnp.stack(outs[7]))
```

```python
import functools
import math

import jax
import jax.numpy as jnp
from jax import lax
from jax.experimental import pallas as pl
from jax.experimental.pallas import tpu as pltpu

F32 = jnp.float32
BF16 = jnp.bfloat16

H_R = 4
DK_R = 128
H_F = 8
DH_F = 64
D_RET = H_R * DK_R
D_FOX = H_F * DH_F
PAGE = 128
RET_CHUNK = 128
ROPE_BASE = 10000.0
EPS = 1e-6
NEG = -1e30
LOG2E = math.log2(math.e)

LANES = 128
SUBLANES = 8
VMEM_LIMIT = 56 * 1024 * 1024

NT_DIMS = (((1,), (1,)), ((), ()))
TN_DIMS = (((0,), (0,)), ((), ()))


def _cparams(sem):
    return pltpu.CompilerParams(dimension_semantics=sem, vmem_limit_bytes=VMEM_LIMIT)


def _seg_cumsum(x, seg):
    lane = lax.broadcasted_iota(jnp.int32, x.shape, 1)
    pos = lane & (seg - 1)
    d = 1
    while d < seg:
        shifted = pltpu.roll(x, d, axis=1)
        x = x + jnp.where(pos >= d, shifted, 0.0)
        d *= 2
    return x


def _proj_kernel(x_ref, ng_ref, w_ref, wkt_ref, wvt_ref, wff_ref, bf_ref, cos_ref, sin_ref, qg_ref, kgt_ref,
                 ones_ref, kstack_ref, vstack_ref,
                 rq_ref, rk_ref, rv_ref, srg_ref, fq_ref, fkt_ref, fktb_ref, fvt_ref, fvb_ref, sfg_ref,
                 lf_ref, c_ref, carry_ref, *, seg, tiles_per_seq):
    del kstack_ref, vstack_ref
    i = pl.program_id(0)

    @pl.when(i % tiles_per_seq == 0)
    def _():
        carry_ref[...] = jnp.zeros_like(carry_ref)

    x = x_ref[...]
    ms = jnp.mean(x * x, axis=-1, keepdims=True)
    h = (x * lax.rsqrt(ms + EPS) * ng_ref[...]).astype(BF16)

    def zcol(g):
        return jnp.dot(h, w_ref[:, g * D_RET:(g + 1) * D_RET], preferred_element_type=F32)

    cos = cos_ref[...]
    sin = sin_ref[...]

    def rope_store(z, out_ref, scale):
        for hh in range(H_R):
            zz = z[:, hh * DK_R:(hh + 1) * DK_R]
            rot = pltpu.roll(zz, DK_R // 2, axis=1)
            r = zz * cos + rot * sin
            if scale is not None:
                r = r * scale
            out_ref[:, hh * DK_R:(hh + 1) * DK_R] = r.astype(out_ref.dtype)

    def silu(z):
        return z / (1.0 + jnp.exp(-z))

    def head_norm(z, gain):
        y = z * z
        yh = y.astype(BF16)
        yl = (y - yh.astype(F32)).astype(BF16)
        ssq = (jnp.dot(yh, ones_ref[...], preferred_element_type=F32)
               + jnp.dot(yl, ones_ref[...], preferred_element_type=F32))
        return z * lax.rsqrt(ssq * (1.0 / DH_F) + EPS) * gain

    rope_store(zcol(0), rq_ref, None)
    rope_store(zcol(1), rk_ref, DK_R ** -0.5)
    rv_ref[...] = zcol(2).astype(BF16)
    srg_ref[...] = silu(zcol(3)).astype(BF16)
    fq_ref[...] = head_norm(zcol(4), qg_ref[...]).astype(BF16)
    zk = lax.dot_general(wkt_ref[...], h, NT_DIMS, preferred_element_type=F32)
    y = zk * zk
    yh = y.astype(BF16)
    yl = (y - yh.astype(F32)).astype(BF16)
    ssq = (jnp.dot(ones_ref[...], yh, preferred_element_type=F32)
           + jnp.dot(ones_ref[...], yl, preferred_element_type=F32))
    kgt = jnp.concatenate([kgt_ref[...]] * (zk.shape[1] // LANES), axis=1)
    fkt = zk * lax.rsqrt(ssq * (1.0 / DH_F) + EPS) * kgt
    fkt_ref[...] = fkt
    fktb_ref[...] = fkt.astype(BF16)
    fvt_ref[...] = lax.dot_general(wvt_ref[...], h, NT_DIMS, preferred_element_type=F32)
    fvb_ref[...] = zcol(6).astype(BF16)
    sfg_ref[...] = silu(zcol(7)).astype(BF16)

    ff = lax.dot_general(wff_ref[...], h, NT_DIMS, preferred_element_type=F32) + bf_ref[...]
    lf = jnp.minimum(ff, 0.0) - jnp.log1p(jnp.exp(-jnp.abs(ff)))
    lf_ref[...] = lf
    c_ref[...] = _seg_cumsum(lf, seg) + carry_ref[...]
    carry_ref[...] = carry_ref[...] + jnp.sum(lf, axis=1, keepdims=True)


def _proj(x2d, ng, w_main, wk_t, wv_t, wff_t, bf, cos_t, sin_t, qg, kg_t, ones_bd, kstack, vstack,
          *, layer, tm, seg, tiles_per_seq):
    n, d_model = x2d.shape
    nt = n // tm
    n_seq = kstack.shape[1]
    seq_len = kstack.shape[3]
    row = lambda i: (i, 0)
    const = lambda i: (0, 0)
    tab = lambda i: (i % tiles_per_seq, 0)
    col = lambda i: (0, i)
    bf16_out = jax.ShapeDtypeStruct((n, D_RET), BF16)
    t_out = jax.ShapeDtypeStruct((H_F, n), F32)
    blk = pl.BlockSpec((tm, D_RET), row)
    tblk = pl.BlockSpec((H_F, tm), col)
    stack_blk = pl.BlockSpec((None, None, D_FOX, tm),
                             lambda i: (layer, i // tiles_per_seq, 0, i % tiles_per_seq))
    ktb_blk = pl.BlockSpec((None, D_FOX, tm), lambda i: (i // tiles_per_seq, 0, i % tiles_per_seq))
    stack_out = jax.ShapeDtypeStruct(kstack.shape, F32)
    any_spec = pl.BlockSpec(memory_space=pl.ANY)
    return pl.pallas_call(
        functools.partial(_proj_kernel, seg=seg, tiles_per_seq=tiles_per_seq),
        grid=(nt,),
        in_specs=[
            pl.BlockSpec((tm, d_model), row),
            pl.BlockSpec((1, d_model), const),
            pl.BlockSpec(w_main.shape, const),
            pl.BlockSpec(wk_t.shape, const),
            pl.BlockSpec(wv_t.shape, const),
            pl.BlockSpec(wff_t.shape, const),
            pl.BlockSpec((H_F, 1), const),
            pl.BlockSpec((tm, LANES), tab),
            pl.BlockSpec((tm, LANES), tab),
            pl.BlockSpec((1, D_FOX), const),
            pl.BlockSpec((D_FOX, LANES), const),
            pl.BlockSpec((D_FOX, D_FOX), const),
            any_spec,
            any_spec,
        ],
        out_specs=[blk, blk, blk, blk, blk, stack_blk, ktb_blk, stack_blk, blk, blk, tblk, tblk],
        out_shape=[bf16_out, bf16_out, bf16_out, bf16_out, bf16_out, stack_out,
                   jax.ShapeDtypeStruct((n_seq, D_FOX, seq_len), BF16), stack_out,
                   bf16_out, bf16_out, t_out, t_out],
        input_output_aliases={12: 5, 13: 7},
        scratch_shapes=[pltpu.VMEM((H_F, 1), F32)],
        compiler_params=_cparams(("arbitrary",)),
        name="proj",
    )(x2d, ng, w_main, wk_t, wv_t, wff_t, bf, cos_t, sin_t, qg, kg_t, ones_bd, kstack, vstack)


def _ret_kernel(q_ref, k_ref, v_ref, g_ref, s0_ref, dec_ref, qd_ref, kd_ref, sd_ref, rg_ref,
                u_ref, sfin_ref, state_ref, *, chunk, n_chunks, nb):
    j = pl.program_id(1)

    @pl.when(j == 0)
    def _():
        state_ref[...] = s0_ref[...]

    for bb in range(nb):
        for hh in range(H_R):
            cols = slice(hh * DK_R, (hh + 1) * DK_R)
            s = state_ref[bb, hh]
            for ci in range(n_chunks):
                rows = slice(ci * chunk, (ci + 1) * chunk)
                q = q_ref[bb, rows, cols]
                k = k_ref[bb, rows, cols]
                v = v_ref[bb, rows, cols]
                sc = lax.dot_general(q, k, NT_DIMS, preferred_element_type=F32) * dec_ref[hh]
                o = (jnp.dot(sc.astype(BF16), v, preferred_element_type=F32)
                     + jnp.dot(q, s.astype(BF16), preferred_element_type=F32) * qd_ref[hh])
                kdec = (k.astype(F32) * kd_ref[hh]).astype(BF16)
                s = sd_ref[hh] * s + lax.dot_general(kdec, v, TN_DIMS, preferred_element_type=F32)
                ms = jnp.mean(o * o, axis=-1, keepdims=True)
                u = o * lax.rsqrt(ms + EPS) * rg_ref[:, cols] * g_ref[bb, rows, cols].astype(F32)
                u_ref[bb, rows, cols] = u.astype(BF16)
            state_ref[bb, hh] = s

    @pl.when(j == pl.num_programs(1) - 1)
    def _():
        sfin_ref[...] = state_ref[...]


def _retention_tables(chunk, c_valid):
    lg = jnp.log(1.0 - 2.0 ** (-5.0 - jnp.arange(H_R, dtype=F32)))
    idx = jnp.arange(chunk, dtype=F32)
    diff = idx[:, None] - idx[None, :]
    dec = jnp.where(diff >= 0, jnp.exp(jnp.maximum(diff, 0.0)[None] * lg[:, None, None]), 0.0)
    qd = jnp.exp((idx + 1.0)[None, :] * lg[:, None])
    kd = jnp.exp((c_valid - 1.0 - idx)[None, :] * lg[:, None])
    sd = jnp.exp(c_valid * lg)
    qd = jnp.broadcast_to(qd[:, :, None], (H_R, chunk, DK_R))
    kd = jnp.broadcast_to(kd[:, :, None], (H_R, chunk, DK_R))
    sd = jnp.broadcast_to(sd[:, None, None], (H_R, 1, DK_R))
    return dec, qd, kd, sd


def _retention(q, k, v, g, s0, ret_g, *, tile, chunk, c_valid, nb):
    b, t, _ = q.shape
    dec, qd, kd, sd = _retention_tables(chunk, c_valid)
    seq = pl.BlockSpec((nb, tile, D_RET), lambda bi, j: (bi, j, 0))
    st = pl.BlockSpec((nb, H_R, DK_R, DK_R), lambda bi, j: (bi, 0, 0, 0))

    def full(a):
        return pl.BlockSpec(a.shape, lambda bi, j: (0,) * a.ndim)

    return pl.pallas_call(
        functools.partial(_ret_kernel, chunk=chunk, n_chunks=tile // chunk, nb=nb),
        grid=(b // nb, t // tile),
        in_specs=[seq, seq, seq, seq, st, full(dec), full(qd), full(kd), full(sd), full(ret_g)],
        out_specs=[seq, st],
        out_shape=[jax.ShapeDtypeStruct((b, t, D_RET), BF16),
                   jax.ShapeDtypeStruct((b, H_R, DK_R, DK_R), F32)],
        scratch_shapes=[pltpu.VMEM((nb, H_R, DK_R, DK_R), F32)],
        compiler_params=_cparams(("arbitrary", "arbitrary")),
        name="retention",
    )(q, k, v, g, s0, dec, qd, kd, sd, ret_g)


def _attn_kernel(q_ref, kt_ref, v_ref, c_ref, g_ref, o_ref, m_ref, acc_ref, *, tq, tk, seq):
    pair = pl.program_id(1)
    nchunk = tk // LANES
    first_q = lax.broadcasted_iota(jnp.int32, (tq, LANES), 1) < DH_F
    first_k = lax.broadcasted_iota(jnp.int32, (tk, LANES), 1) < DH_F

    def q_block(i, carry):
        q0 = pl.multiple_of(i * tq, tq)
        q = q_ref[pl.ds(q0, tq), :]
        zero = jnp.zeros_like(q)
        qs = (jnp.where(first_q, q, zero), jnp.where(first_q, zero, q))
        m_ref[...] = jnp.full(m_ref.shape, NEG, F32)
        acc_ref[...] = jnp.zeros_like(acc_ref)

        def kv_block(j, row0, masked):
            k0 = pl.multiple_of(j * tk, tk)
            kt = kt_ref[:, pl.ds(k0, tk)]
            vb = v_ref[pl.ds(k0, tk), :]
            one = jnp.ones_like(vb)
            vs = (jnp.where(first_k, vb, one), jnp.where(first_k, one, vb))
            for hh in range(2):
                ck = c_ref[pl.ds(2 * pair + hh, 1), pl.ds(k0, tk)] * LOG2E
                s = jnp.dot(qs[hh][row0:], kt, preferred_element_type=F32)
                chunks = []
                for c in range(nchunk):
                    ch = s[:, c * LANES:(c + 1) * LANES] - ck[:, c * LANES:(c + 1) * LANES]
                    if masked:
                        rowi = lax.broadcasted_iota(jnp.int32, ch.shape, 0)
                        coli = lax.broadcasted_iota(jnp.int32, ch.shape, 1)
                        ch = jnp.where(rowi >= coli + c * LANES, ch, NEG)
                    chunks.append(ch)
                mx = chunks[0]
                for c in range(1, nchunk):
                    mx = jnp.maximum(mx, chunks[c])
                m_prev = m_ref[hh, row0:, :]
                m_new = jnp.maximum(m_prev, jnp.max(mx, axis=1, keepdims=True))
                alpha = jnp.exp2(m_prev - m_new)
                p = jnp.concatenate([jnp.exp2(ch - m_new) for ch in chunks], axis=1).astype(BF16)
                acc_ref[hh, row0:, :] = (acc_ref[hh, row0:, :] * alpha
                                         + jnp.dot(p, vs[hh], preferred_element_type=F32))
                m_ref[hh, row0:, :] = m_new

        r = tq // tk
        n_full = i * r

        def body(j, c2):
            kv_block(j, 0, False)
            return c2

        lax.fori_loop(0, n_full, body, 0)
        for jj in range(r):
            kv_block(n_full + jj, jj * tk, True)

        acc_a = acc_ref[0]
        acc_b = acc_ref[1]
        num = jnp.where(first_q, acc_a, acc_b)
        den = jnp.where(first_q, pltpu.roll(acc_a, DH_F, axis=1), pltpu.roll(acc_b, DH_F, axis=1))
        o_ref[pl.ds(q0, tq), :] = (num / den * g_ref[pl.ds(q0, tq), :].astype(F32)).astype(BF16)
        return carry

    lax.fori_loop(0, seq // tq, q_block, 0)


def _attention(fq, fkt, fvb, c_t, sfg, *, batch, seq, tq, tk):
    n = batch * seq
    blk = pl.BlockSpec((seq, LANES), lambda b, p: (b, p))
    ktblk = pl.BlockSpec((None, LANES, seq), lambda b, p: (b, p, 0))
    cblk = pl.BlockSpec((H_F, seq), lambda b, p: (0, b))
    return pl.pallas_call(
        functools.partial(_attn_kernel, tq=tq, tk=tk, seq=seq),
        grid=(batch, H_F // 2),
        in_specs=[blk, ktblk, blk, cblk, blk],
        out_specs=blk,
        out_shape=jax.ShapeDtypeStruct((n, D_FOX), BF16),
        scratch_shapes=[pltpu.VMEM((2, tq, LANES), F32), pltpu.VMEM((2, tq, LANES), F32)],
        compiler_params=_cparams(("arbitrary", "arbitrary")),
        name="attention",
    )(fq, fkt, fvb, c_t, sfg)


def _bias_kernel(pt_ref, lf_hbm, tri_ref, out_ref, buf, sem, *, n_pages):
    li = pl.program_id(0)
    b = pl.program_id(1)

    def page_copy(p):
        return pltpu.make_async_copy(lf_hbm.at[li, pt_ref[b, p]], buf.at[p], sem)

    def issue(p, carry):
        page_copy(p).start()
        return carry

    def wait(p, carry):
        page_copy(p).wait()
        return carry

    lax.fori_loop(0, n_pages, issue, 0)
    lax.fori_loop(0, n_pages, wait, 0)

    x = buf[...].reshape(n_pages * H_F, PAGE)
    x_hi = x.astype(BF16)
    r1 = x - x_hi.astype(F32)
    x_mid = r1.astype(BF16)
    x_lo = (r1 - x_mid.astype(F32)).astype(BF16)
    tri = tri_ref[...]
    s = (jnp.dot(x_hi, tri, preferred_element_type=F32) + jnp.dot(x_mid, tri, preferred_element_type=F32)
         + jnp.dot(x_lo, tri, preferred_element_type=F32))
    s3 = s.reshape(n_pages, H_F, PAGE)
    tot = jnp.broadcast_to(s3[:, :, 0:1], s3.shape)
    e = tot
    d = 1
    while d < n_pages:
        e = e + jnp.concatenate([e[d:], jnp.zeros((d,) + e.shape[1:], F32)], axis=0)
        d *= 2
    res = s3 - buf[...] + (e - tot)
    for p in range(n_pages):
        out_ref[:, p * PAGE:(p + 1) * PAGE] = res[p]


def _past_bias(lf_t, page_table):
    depth = lf_t.shape[0]
    db, n_pages = page_table.shape
    grid_spec = pltpu.PrefetchScalarGridSpec(
        num_scalar_prefetch=1,
        grid=(depth, db),
        in_specs=[pl.BlockSpec(memory_space=pl.ANY), pl.BlockSpec((PAGE, PAGE), lambda li, b, pt: (0, 0))],
        out_specs=pl.BlockSpec((None, None, H_F, n_pages * PAGE), lambda li, b, pt: (li, b, 0, 0)),
        scratch_shapes=[pltpu.VMEM((n_pages, H_F, PAGE), F32), pltpu.SemaphoreType.DMA(())],
    )
    idx = jnp.arange(PAGE)
    tri = (idx[:, None] >= idx[None, :]).astype(BF16)
    return pl.pallas_call(
        functools.partial(_bias_kernel, n_pages=n_pages),
        grid_spec=grid_spec,
        out_shape=jax.ShapeDtypeStruct((depth, db, H_F, n_pages * PAGE), F32),
        compiler_params=_cparams(("arbitrary", "arbitrary")),
        name="past_bias",
    )(page_table, lf_t, tri)


def _decode_kernel(pt_ref, q_ref, kn_ref, vn_ref, bn_ref, bias_ref, g_ref, *rest, npg, t_new):
    k_refs = rest[:npg]
    v_refs = rest[npg:2 * npg]
    o_ref, m_ref, l_ref, acc_ref = rest[2 * npg:]
    c = pl.program_id(1)
    q = q_ref[...]
    rows = q.shape[0]

    @pl.when(c == 0)
    def _():
        s = lax.dot_general(q, kn_ref[...], NT_DIMS, preferred_element_type=F32) + bn_ref[...]
        m = jnp.max(s, axis=1, keepdims=True)
        p = jnp.exp2(s - m)
        m_ref[...] = m
        l_ref[...] = jnp.sum(p, axis=1, keepdims=True)
        acc_ref[...] = jnp.dot(p.astype(BF16), vn_ref[...], preferred_element_type=F32)

    ss = []
    for p in range(npg):
        kt = k_refs[p][...].reshape(D_FOX, PAGE).astype(BF16)
        b8 = bias_ref[:, p * PAGE:(p + 1) * PAGE] * LOG2E
        s = jnp.dot(q, kt, preferred_element_type=F32)
        ss.append(s + jnp.concatenate([b8] * t_new, axis=0))
    m_cur = ss[0]
    for p in range(1, npg):
        m_cur = jnp.maximum(m_cur, ss[p])
    m_prev = m_ref[...]
    m_new = jnp.maximum(m_prev, jnp.max(m_cur, axis=1, keepdims=True))
    alpha = jnp.exp2(m_prev - m_new)
    acc = alpha * acc_ref[...]
    lsum = jnp.zeros((rows, PAGE), F32)
    for p in range(npg):
        pp = jnp.exp2(ss[p] - m_new)
        lsum = lsum + pp
        vt = v_refs[p][...].reshape(D_FOX, PAGE).astype(BF16)
        acc = acc + lax.dot_general(pp.astype(BF16), vt, NT_DIMS, preferred_element_type=F32)
    m_ref[...] = m_new
    l_ref[...] = alpha * l_ref[...] + jnp.sum(lsum, axis=1, keepdims=True)
    acc_ref[...] = acc

    @pl.when(c == pl.num_programs(1) - 1)
    def _():
        o = acc_ref[...] / l_ref[...]
        rowh = lax.broadcasted_iota(jnp.int32, (rows, D_FOX), 0) % H_F
        colh = lax.broadcasted_iota(jnp.int32, (rows, D_FOX), 1) // DH_F
        o = jnp.where(rowh == colh, o, 0.0)
        o4 = jnp.sum(o.reshape(t_new, H_F, D_FOX), axis=1)
        o_ref[...] = o4 * g_ref[...]


def _decode(page_table, qbd, kn, vn, bias_new, bias_past, sfg, cache_kt, cache_vt, *, layer, npg, t_new):
    db, n_pages = page_table.shape
    rows = qbd.shape[1]
    per_b = lambda a: pl.BlockSpec((None,) + a.shape[1:], lambda b, c, pt: (b,) + (0,) * (a.ndim - 1))

    def page_spec(p):
        return pl.BlockSpec((None, None, H_F, DH_F, PAGE),
                            lambda b, c, pt: (layer, pt[b, c * npg + p], 0, 0, 0))

    grid_spec = pltpu.PrefetchScalarGridSpec(
        num_scalar_prefetch=1,
        grid=(db, n_pages // npg),
        in_specs=([per_b(qbd), per_b(kn), per_b(vn), per_b(bias_new),
                   pl.BlockSpec((None, None, H_F, npg * PAGE), lambda b, c, pt: (layer, b, 0, c)),
                   per_b(sfg)]
                  + [page_spec(p) for p in range(npg)] + [page_spec(p) for p in range(npg)]),
        out_specs=pl.BlockSpec((None, t_new, D_FOX), lambda b, c, pt: (b, 0, 0)),
        scratch_shapes=[pltpu.VMEM((rows, 1), F32), pltpu.VMEM((rows, 1), F32),
                        pltpu.VMEM((rows, D_FOX), F32)],
    )
    return pl.pallas_call(
        functools.partial(_decode_kernel, npg=npg, t_new=t_new),
        grid_spec=grid_spec,
        out_shape=jax.ShapeDtypeStruct((db, t_new, D_FOX), F32),
        compiler_params=_cparams(("arbitrary", "arbitrary")),
        name="decode",
    )(page_table, qbd, kn, vn, bias_new, bias_past, sfg, *([cache_kt] * npg), *([cache_vt] * npg))


def _merge_kernel(x_ref, ur_ref, uf_ref, w_ref, y_ref):
    y_ref[...] = (x_ref[...]
                  + jnp.dot(ur_ref[...], w_ref[:D_RET, :], preferred_element_type=F32)
                  + jnp.dot(uf_ref[...], w_ref[D_RET:, :], preferred_element_type=F32))


def _merge(x2d, u_ret, u_fox, w_out, *, tm):
    n, d_model = x2d.shape
    row = lambda i: (i, 0)
    return pl.pallas_call(
        _merge_kernel,
        grid=(n // tm,),
        in_specs=[pl.BlockSpec((tm, d_model), row), pl.BlockSpec((tm, D_RET), row),
                  pl.BlockSpec((tm, D_FOX), row), pl.BlockSpec(w_out.shape, lambda i: (0, 0))],
        out_specs=pl.BlockSpec((tm, d_model), row),
        out_shape=jax.ShapeDtypeStruct((n, d_model), F32),
        compiler_params=_cparams(("parallel",)),
        name="merge",
    )(x2d, u_ret, u_fox, w_out)


def _rope_tables(pos):
    inv = 1.0 / (ROPE_BASE ** (jnp.arange(0, DK_R, 2, dtype=F32) / DK_R))
    ang = pos[:, None] * inv[None, :]
    cos = jnp.cos(ang)
    sin = jnp.sin(ang)
    return jnp.concatenate([cos, cos], axis=-1), jnp.concatenate([-sin, sin], axis=-1)


def kernel(x_prompt, x_sample, cache_fox_k, cache_fox_v, cache_fox_logf, state_ret, page_table,
           norm_gain, w_in, ret_norm_gain, fox_q_gain, fox_k_gain, fox_b_f, w_out):
    batch, seq, d_model = x_prompt.shape
    db, t_new, _ = x_sample.shape
    depth = w_in.shape[0]
    n_pool = cache_fox_k.shape[1]
    n_pages = page_table.shape[1]
    past_len = n_pages * PAGE
    n_p = batch * seq
    n_s = db * t_new
    pad_new = 16
    rows_s = t_new * H_F

    tm_p = min(512, n_p)
    tq = min(1024, seq)
    tk = min(512, seq)
    ret_tile = min(512, seq)
    ret_chunk = min(2 * RET_CHUNK, seq)
    npg = min(32, n_pages)

    w_main = w_in[:, :, :4 * D_RET + 4 * D_FOX].astype(BF16)
    k_col = 4 * D_RET + D_FOX
    wk_t = jnp.swapaxes(w_in[:, :, k_col:k_col + D_FOX], 1, 2).astype(BF16)
    wv_t = jnp.swapaxes(w_in[:, :, k_col + D_FOX:k_col + 2 * D_FOX], 1, 2).astype(BF16)
    wff_t = jnp.swapaxes(w_in[:, :, 4 * D_RET + 4 * D_FOX:], 1, 2).astype(BF16)
    w_out_b = w_out.astype(BF16)
    head_id = jnp.arange(D_FOX, dtype=jnp.int32) // DH_F
    ones_bd = (head_id[:, None] == head_id[None, :]).astype(BF16)
    cos_p, sin_p = _rope_tables(jnp.arange(seq, dtype=F32))
    pos_s = past_len + jnp.arange(t_new, dtype=F32)
    cos_s, sin_s = _rope_tables(jnp.tile(pos_s, db))
    qg = jnp.tile(fox_q_gain, (1, H_F)) * (DH_F ** -0.5 * LOG2E)
    kg_t = jnp.broadcast_to(jnp.tile(fox_k_gain, (1, H_F))[:, :, None], (depth, D_FOX, LANES))
    ret_g = ret_norm_gain.reshape(depth, 1, D_RET)
    kstack_p = jnp.zeros((depth, batch, D_FOX, seq), F32)
    vstack_p = jnp.zeros((depth, batch, D_FOX, seq), F32)

    cache_kt = jnp.transpose(cache_fox_k, (0, 1, 3, 4, 2))
    cache_vt = jnp.transpose(cache_fox_v, (0, 1, 3, 4, 2))
    lf_t = jnp.swapaxes(cache_fox_logf, 2, 3)
    bias_past = _past_bias(lf_t, page_table)

    eye_h = jnp.eye(H_F, dtype=BF16)
    jj = jnp.arange(pad_new)
    new_valid = (jj[None, :] <= jnp.arange(t_new)[:, None]) & (jj[None, :] < t_new)
    s0_p = jnp.zeros((batch, H_R, DK_R, DK_R), F32)

    def pad_rows(a):
        a = a.reshape(db, t_new, a.shape[-1])
        return jnp.pad(a, ((0, 0), (0, pad_new - t_new), (0, 0)))

    xp = x_prompt.reshape(n_p, d_model)
    xs = x_sample.reshape(n_s, d_model)
    outs = [[] for _ in range(8)]
    for l in range(depth):
        proj_args = (norm_gain[l][None, :], w_main[l], wk_t[l], wv_t[l], wff_t[l], fox_b_f[l][:, None])
        gains = (qg[l][None, :], kg_t[l], ones_bd)

        (rq, rk, rv, srg, fq, kstack_p, fkt, vstack_p, fvb, sfg, lf, c_t) = _proj(
            xp, *proj_args, cos_p, sin_p, *gains, kstack_p, vstack_p,
            layer=l, tm=tm_p, seg=tm_p, tiles_per_seq=seq // tm_p)
        r3 = lambda a: a.reshape(batch, seq, D_RET)
        u_ret, s_p = _retention(r3(rq), r3(rk), r3(rv), r3(srg), s0_p, ret_g[l],
                                tile=ret_tile, chunk=ret_chunk, c_valid=float(ret_chunk), nb=1)
        u_fox = _attention(fq, fkt, fvb, c_t, sfg, batch=batch, seq=seq, tq=tq, tk=tk)
        xp = _merge(xp, u_ret.reshape(n_p, D_RET), u_fox, w_out_b[l], tm=tm_p)
        outs[2].append(lf.T.reshape(batch, seq, H_F))
        outs[3].append(s_p)

        one_slab = jnp.zeros((1, 1, D_FOX, n_s), F32)
        (rq, rk, rv, srg, fq, fkt_s, fktb_s, fvt_s, fvb, sfg, lf, c_t) = _proj(
            xs, *proj_args, cos_s, sin_s, *gains, one_slab, one_slab,
            layer=0, tm=n_s, seg=t_new, tiles_per_seq=1)
        fk = fkt_s[0, 0].T
        fv = fvt_s[0, 0].T
        fkb = fktb_s[0].T
        u_ret, s_s = _retention(pad_rows(rq), pad_rows(rk), pad_rows(rv), pad_rows(srg), state_ret[l], ret_g[l],
                                tile=pad_new, chunk=pad_new, c_valid=float(t_new), nb=math.gcd(db, 4))
        u_ret = u_ret[:, :t_new].reshape(n_s, D_RET)
        q5 = fq.reshape(db, t_new, 1, H_F, DH_F) * eye_h[None, None, :, :, None]
        qbd = q5.reshape(db, rows_s, D_FOX)
        cn = c_t.reshape(H_F, db, t_new).transpose(1, 0, 2)
        bn = jnp.pad(-cn * LOG2E, ((0, 0), (0, 0), (0, pad_new - t_new)))
        bn = jnp.where(new_valid[None, :, None, :], bn[:, None, :, :], NEG)
        bn = bn.reshape(db, rows_s, pad_new)
        u_fox = _decode(page_table, qbd, pad_rows(fkb), pad_rows(fvb), bn, bias_past,
                        sfg.astype(F32).reshape(db, t_new, D_FOX), cache_kt, cache_vt,
                        layer=l, npg=npg, t_new=t_new)
        xs = _merge(xs, u_ret, u_fox.reshape(n_s, D_FOX).astype(BF16), w_out_b[l], tm=n_s)
        outs[4].append(fk.reshape(db, t_new, H_F, DH_F))
        outs[5].append(fv.reshape(db, t_new, H_F, DH_F))
        outs[6].append(lf.T.reshape(db, t_new, H_F))
        outs[7].append(s_s)

    def untranspose(stack):
        return jnp.transpose(stack.reshape(depth, batch, H_F, DH_F, seq), (0, 1, 4, 2, 3))

    return (xp.reshape(batch, seq, d_model), xs.reshape(db, t_new, d_model),
            untranspose(kstack_p), untranspose(vstack_p), jnp.stack(outs[2]), jnp.stack(outs[3]),
            jnp.stack(outs[4]), jnp.stack(outs[5]), jnp.stack(outs[6]), jnp.stack(outs[7]))
```

```python
import functools
import math

import jax
import jax.numpy as jnp
from jax import lax
from jax.experimental import pallas as pl
from jax.experimental.pallas import tpu as pltpu

F32 = jnp.float32
BF16 = jnp.bfloat16

H_R = 4
DK_R = 128
H_F = 8
DH_F = 64
D_RET = H_R * DK_R
D_FOX = H_F * DH_F
PAGE = 128
RET_CHUNK = 128
ROPE_BASE = 10000.0
EPS = 1e-6
NEG = -1e30
LOG2E = math.log2(math.e)

LANES = 128
SUBLANES = 8
VMEM_LIMIT = 56 * 1024 * 1024

NT_DIMS = (((1,), (1,)), ((), ()))
TN_DIMS = (((0,), (0,)), ((), ()))


def _cparams(sem):
    return pltpu.CompilerParams(dimension_semantics=sem, vmem_limit_bytes=VMEM_LIMIT)


def _seg_cumsum(x, seg):
    lane = lax.broadcasted_iota(jnp.int32, x.shape, 1)
    pos = lane & (seg - 1)
    d = 1
    while d < seg:
        shifted = pltpu.roll(x, d, axis=1)
        x = x + jnp.where(pos >= d, shifted, 0.0)
        d *= 2
    return x


def _proj_kernel(x_ref, ng_ref, w_ref, wkt_ref, wvt_ref, wff_ref, bf_ref, cos_ref, sin_ref, qg_ref, kgt_ref,
                 ones_ref, kstack_ref, vstack_ref,
                 rq_ref, rk_ref, rv_ref, srg_ref, fq_ref, fkt_ref, fktb_ref, fvt_ref, fvb_ref, sfg_ref,
                 lf_ref, c_ref, carry_ref, *, seg, tiles_per_seq):
    del kstack_ref, vstack_ref
    i = pl.program_id(0)

    @pl.when(i % tiles_per_seq == 0)
    def _():
        carry_ref[...] = jnp.zeros_like(carry_ref)

    x = x_ref[...]
    ms = jnp.mean(x * x, axis=-1, keepdims=True)
    h = (x * lax.rsqrt(ms + EPS) * ng_ref[...]).astype(BF16)

    def zcol(g):
        return jnp.dot(h, w_ref[:, g * D_RET:(g + 1) * D_RET], preferred_element_type=F32)

    cos = cos_ref[...]
    sin = sin_ref[...]

    def rope_store(z, out_ref, scale):
        for hh in range(H_R):
            zz = z[:, hh * DK_R:(hh + 1) * DK_R]
            rot = pltpu.roll(zz, DK_R // 2, axis=1)
            r = zz * cos + rot * sin
            if scale is not None:
                r = r * scale
            out_ref[:, hh * DK_R:(hh + 1) * DK_R] = r.astype(out_ref.dtype)

    def silu(z):
        return z / (1.0 + jnp.exp(-z))

    def head_norm(z, gain):
        ssq = jnp.dot((z * z).astype(BF16), ones_ref[...], preferred_element_type=F32)
        return z * lax.rsqrt(ssq * (1.0 / DH_F) + EPS) * gain

    rope_store(zcol(0), rq_ref, None)
    rope_store(zcol(1), rk_ref, DK_R ** -0.5)
    rv_ref[...] = zcol(2).astype(BF16)
    srg_ref[...] = silu(zcol(3)).astype(BF16)
    fq_ref[...] = head_norm(zcol(4), qg_ref[...]).astype(BF16)
    zk = lax.dot_general(wkt_ref[...], h, NT_DIMS, preferred_element_type=F32)
    ssq = jnp.dot(ones_ref[...], (zk * zk).astype(BF16), preferred_element_type=F32)
    kgt = jnp.concatenate([kgt_ref[...]] * (zk.shape[1] // LANES), axis=1)
    fkt = zk * lax.rsqrt(ssq * (1.0 / DH_F) + EPS) * kgt
    fkt_ref[...] = fkt
    fktb_ref[...] = fkt.astype(BF16)
    fvt_ref[...] = lax.dot_general(wvt_ref[...], h, NT_DIMS, preferred_element_type=F32)
    fvb_ref[...] = zcol(6).astype(BF16)
    sfg_ref[...] = silu(zcol(7)).astype(BF16)

    ff = lax.dot_general(wff_ref[...], h, NT_DIMS, preferred_element_type=F32) + bf_ref[...]
    lf = jnp.minimum(ff, 0.0) - jnp.log1p(jnp.exp(-jnp.abs(ff)))
    lf_ref[...] = lf
    c_ref[...] = _seg_cumsum(lf, seg) + carry_ref[...]
    carry_ref[...] = carry_ref[...] + jnp.sum(lf, axis=1, keepdims=True)


def _proj(x2d, ng, w_main, wk_t, wv_t, wff_t, bf, cos_t, sin_t, qg, kg_t, ones_bd, kstack, vstack,
          *, layer, tm, seg, tiles_per_seq):
    n, d_model = x2d.shape
    nt = n // tm
    n_seq = kstack.shape[1]
    seq_len = kstack.shape[3]
    row = lambda i: (i, 0)
    const = lambda i: (0, 0)
    tab = lambda i: (i % tiles_per_seq, 0)
    col = lambda i: (0, i)
    bf16_out = jax.ShapeDtypeStruct((n, D_RET), BF16)
    t_out = jax.ShapeDtypeStruct((H_F, n), F32)
    blk = pl.BlockSpec((tm, D_RET), row)
    tblk = pl.BlockSpec((H_F, tm), col)
    stack_blk = pl.BlockSpec((None, None, D_FOX, tm),
                             lambda i: (layer, i // tiles_per_seq, 0, i % tiles_per_seq))
    ktb_blk = pl.BlockSpec((None, D_FOX, tm), lambda i: (i // tiles_per_seq, 0, i % tiles_per_seq))
    stack_out = jax.ShapeDtypeStruct(kstack.shape, F32)
    any_spec = pl.BlockSpec(memory_space=pl.ANY)
    return pl.pallas_call(
        functools.partial(_proj_kernel, seg=seg, tiles_per_seq=tiles_per_seq),
        grid=(nt,),
        in_specs=[
            pl.BlockSpec((tm, d_model), row),
            pl.BlockSpec((1, d_model), const),
            pl.BlockSpec(w_main.shape, const),
            pl.BlockSpec(wk_t.shape, const),
            pl.BlockSpec(wv_t.shape, const),
            pl.BlockSpec(wff_t.shape, const),
            pl.BlockSpec((H_F, 1), const),
            pl.BlockSpec((tm, LANES), tab),
            pl.BlockSpec((tm, LANES), tab),
            pl.BlockSpec((1, D_FOX), const),
            pl.BlockSpec((D_FOX, LANES), const),
            pl.BlockSpec((D_FOX, D_FOX), const),
            any_spec,
            any_spec,
        ],
        out_specs=[blk, blk, blk, blk, blk, stack_blk, ktb_blk, stack_blk, blk, blk, tblk, tblk],
        out_shape=[bf16_out, bf16_out, bf16_out, bf16_out, bf16_out, stack_out,
                   jax.ShapeDtypeStruct((n_seq, D_FOX, seq_len), BF16), stack_out,
                   bf16_out, bf16_out, t_out, t_out],
        input_output_aliases={12: 5, 13: 7},
        scratch_shapes=[pltpu.VMEM((H_F, 1), F32)],
        compiler_params=_cparams(("arbitrary",)),
        name="proj",
    )(x2d, ng, w_main, wk_t, wv_t, wff_t, bf, cos_t, sin_t, qg, kg_t, ones_bd, kstack, vstack)


def _ret_kernel(q_ref, k_ref, v_ref, g_ref, s0_ref, dec_ref, qd_ref, kd_ref, sd_ref, rg_ref,
                u_ref, sfin_ref, state_ref, *, chunk, n_chunks, nb):
    j = pl.program_id(1)

    @pl.when(j == 0)
    def _():
        state_ref[...] = s0_ref[...]

    for bb in range(nb):
        for hh in range(H_R):
            cols = slice(hh * DK_R, (hh + 1) * DK_R)
            s = state_ref[bb, hh]
            for ci in range(n_chunks):
                rows = slice(ci * chunk, (ci + 1) * chunk)
                q = q_ref[bb, rows, cols]
                k = k_ref[bb, rows, cols]
                v = v_ref[bb, rows, cols]
                sc = lax.dot_general(q, k, NT_DIMS, preferred_element_type=F32) * dec_ref[hh]
                o = (jnp.dot(sc.astype(BF16), v, preferred_element_type=F32)
                     + jnp.dot(q, s.astype(BF16), preferred_element_type=F32) * qd_ref[hh])
                kdec = (k.astype(F32) * kd_ref[hh]).astype(BF16)
                s = sd_ref[hh] * s + lax.dot_general(kdec, v, TN_DIMS, preferred_element_type=F32)
                ms = jnp.mean(o * o, axis=-1, keepdims=True)
                u = o * lax.rsqrt(ms + EPS) * rg_ref[:, cols] * g_ref[bb, rows, cols].astype(F32)
                u_ref[bb, rows, cols] = u.astype(BF16)
            state_ref[bb, hh] = s

    @pl.when(j == pl.num_programs(1) - 1)
    def _():
        sfin_ref[...] = state_ref[...]


def _retention_tables(chunk, c_valid):
    lg = jnp.log(1.0 - 2.0 ** (-5.0 - jnp.arange(H_R, dtype=F32)))
    idx = jnp.arange(chunk, dtype=F32)
    diff = idx[:, None] - idx[None, :]
    dec = jnp.where(diff >= 0, jnp.exp(jnp.maximum(diff, 0.0)[None] * lg[:, None, None]), 0.0)
    qd = jnp.exp((idx + 1.0)[None, :] * lg[:, None])
    kd = jnp.exp((c_valid - 1.0 - idx)[None, :] * lg[:, None])
    sd = jnp.exp(c_valid * lg)
    qd = jnp.broadcast_to(qd[:, :, None], (H_R, chunk, DK_R))
    kd = jnp.broadcast_to(kd[:, :, None], (H_R, chunk, DK_R))
    sd = jnp.broadcast_to(sd[:, None, None], (H_R, 1, DK_R))
    return dec, qd, kd, sd


def _retention(q, k, v, g, s0, ret_g, *, s0_layer, tile, chunk, c_valid, nb):
    b, t, _ = q.shape
    dec, qd, kd, sd = _retention_tables(chunk, c_valid)
    seq = pl.BlockSpec((nb, tile, D_RET), lambda bi, j: (bi, j, 0))
    st = pl.BlockSpec((nb, H_R, DK_R, DK_R), lambda bi, j: (bi, 0, 0, 0))
    st_in = pl.BlockSpec((None, nb, H_R, DK_R, DK_R), lambda bi, j: (s0_layer, bi, 0, 0, 0))

    def full(a):
        return pl.BlockSpec(a.shape, lambda bi, j: (0,) * a.ndim)

    return pl.pallas_call(
        functools.partial(_ret_kernel, chunk=chunk, n_chunks=tile // chunk, nb=nb),
        grid=(b // nb, t // tile),
        in_specs=[seq, seq, seq, seq, st_in, full(dec), full(qd), full(kd), full(sd), full(ret_g)],
        out_specs=[seq, st],
        out_shape=[jax.ShapeDtypeStruct((b, t, D_RET), BF16),
                   jax.ShapeDtypeStruct((b, H_R, DK_R, DK_R), F32)],
        scratch_shapes=[pltpu.VMEM((nb, H_R, DK_R, DK_R), F32)],
        compiler_params=_cparams(("arbitrary", "arbitrary")),
        name="retention",
    )(q, k, v, g, s0, dec, qd, kd, sd, ret_g)


def _attn_kernel(q_ref, kt_ref, v_ref, c_ref, g_ref, o_ref, m_ref, acc_ref, *, tq, tk, seq):
    pair = pl.program_id(1)
    nchunk = tk // LANES
    first_q = lax.broadcasted_iota(jnp.int32, (tq, LANES), 1) < DH_F
    first_k = lax.broadcasted_iota(jnp.int32, (tk, LANES), 1) < DH_F

    def q_block(i, carry):
        q0 = pl.multiple_of(i * tq, tq)
        q = q_ref[pl.ds(q0, tq), :]
        zero = jnp.zeros_like(q)
        qs = (jnp.where(first_q, q, zero), jnp.where(first_q, zero, q))
        m_ref[...] = jnp.full(m_ref.shape, NEG, F32)
        acc_ref[...] = jnp.zeros_like(acc_ref)

        def kv_block(j, row0, masked):
            k0 = pl.multiple_of(j * tk, tk)
            kt = kt_ref[:, pl.ds(k0, tk)]
            vb = v_ref[pl.ds(k0, tk), :]
            one = jnp.ones_like(vb)
            vs = (jnp.where(first_k, vb, one), jnp.where(first_k, one, vb))
            for hh in range(2):
                ck = c_ref[pl.ds(2 * pair + hh, 1), pl.ds(k0, tk)] * LOG2E
                s = jnp.dot(qs[hh][row0:], kt, preferred_element_type=F32)
                chunks = []
                for c in range(nchunk):
                    ch = s[:, c * LANES:(c + 1) * LANES] - ck[:, c * LANES:(c + 1) * LANES]
                    if masked:
                        rowi = lax.broadcasted_iota(jnp.int32, ch.shape, 0)
                        coli = lax.broadcasted_iota(jnp.int32, ch.shape, 1)
                        ch = jnp.where(rowi >= coli + c * LANES, ch, NEG)
                    chunks.append(ch)
                mx = chunks[0]
                for c in range(1, nchunk):
                    mx = jnp.maximum(mx, chunks[c])
                m_prev = m_ref[hh, row0:, :]
                m_new = jnp.maximum(m_prev, jnp.max(mx, axis=1, keepdims=True))
                alpha = jnp.exp2(m_prev - m_new)
                p = jnp.concatenate([jnp.exp2(ch - m_new) for ch in chunks], axis=1).astype(BF16)
                acc_ref[hh, row0:, :] = (acc_ref[hh, row0:, :] * alpha
                                         + jnp.dot(p, vs[hh], preferred_element_type=F32))
                m_ref[hh, row0:, :] = m_new

        r = tq // tk
        n_full = i * r

        def body(j, c2):
            kv_block(j, 0, False)
            return c2

        lax.fori_loop(0, n_full, body, 0)
        for jj in range(r):
            kv_block(n_full + jj, jj * tk, True)

        acc_a = acc_ref[0]
        acc_b = acc_ref[1]
        num = jnp.where(first_q, acc_a, acc_b)
        den = jnp.where(first_q, pltpu.roll(acc_a, DH_F, axis=1), pltpu.roll(acc_b, DH_F, axis=1))
        o_ref[pl.ds(q0, tq), :] = (num / den * g_ref[pl.ds(q0, tq), :].astype(F32)).astype(BF16)
        return carry

    lax.fori_loop(0, seq // tq, q_block, 0)


def _attention(fq, fkt, fvb, c_t, sfg, *, batch, seq, tq, tk):
    n = batch * seq
    blk = pl.BlockSpec((seq, LANES), lambda b, p: (b, p))
    ktblk = pl.BlockSpec((None, LANES, seq), lambda b, p: (b, p, 0))
    cblk = pl.BlockSpec((H_F, seq), lambda b, p: (0, b))
    return pl.pallas_call(
        functools.partial(_attn_kernel, tq=tq, tk=tk, seq=seq),
        grid=(batch, H_F // 2),
        in_specs=[blk, ktblk, blk, cblk, blk],
        out_specs=blk,
        out_shape=jax.ShapeDtypeStruct((n, D_FOX), BF16),
        scratch_shapes=[pltpu.VMEM((2, tq, LANES), F32), pltpu.VMEM((2, tq, LANES), F32)],
        compiler_params=_cparams(("arbitrary", "arbitrary")),
        name="attention",
    )(fq, fkt, fvb, c_t, sfg)


def _bias_kernel(pt_ref, tri_ref, *rest, n_pages):
    del pt_ref
    page_refs = rest[:n_pages]
    out_ref = rest[n_pages]
    x3 = jnp.stack([r[...] for r in page_refs], axis=0)

    x = x3.reshape(n_pages * H_F, PAGE)
    x_hi = x.astype(BF16)
    r1 = x - x_hi.astype(F32)
    x_mid = r1.astype(BF16)
    x_lo = (r1 - x_mid.astype(F32)).astype(BF16)
    tri = tri_ref[...]
    s = (jnp.dot(x_hi, tri, preferred_element_type=F32) + jnp.dot(x_mid, tri, preferred_element_type=F32)
         + jnp.dot(x_lo, tri, preferred_element_type=F32))
    s3 = s.reshape(n_pages, H_F, PAGE)
    tot = jnp.broadcast_to(s3[:, :, 0:1], s3.shape)
    e = tot
    d = 1
    while d < n_pages:
        e = e + jnp.concatenate([e[d:], jnp.zeros((d,) + e.shape[1:], F32)], axis=0)
        d *= 2
    res = s3 - x3 + (e - tot)
    for p in range(n_pages):
        out_ref[:, p * PAGE:(p + 1) * PAGE] = res[p]


def _past_bias(lf_t, page_table):
    depth = lf_t.shape[0]
    db, n_pages = page_table.shape

    def page_index(p, li, b, pt):
        return (li, pt[b, p], 0, 0)

    grid_spec = pltpu.PrefetchScalarGridSpec(
        num_scalar_prefetch=1,
        grid=(depth, db),
        in_specs=([pl.BlockSpec((PAGE, PAGE), lambda li, b, pt: (0, 0))]
                  + [pl.BlockSpec((None, None, H_F, PAGE), functools.partial(page_index, p))
                     for p in range(n_pages)]),
        out_specs=pl.BlockSpec((None, None, H_F, n_pages * PAGE), lambda li, b, pt: (li, b, 0, 0)),
    )
    idx = jnp.arange(PAGE)
    tri = (idx[:, None] >= idx[None, :]).astype(BF16)
    return pl.pallas_call(
        functools.partial(_bias_kernel, n_pages=n_pages),
        grid_spec=grid_spec,
        out_shape=jax.ShapeDtypeStruct((depth, db, H_F, n_pages * PAGE), F32),
        compiler_params=_cparams(("arbitrary", "arbitrary")),
        name="past_bias",
    )(page_table, tri, *([lf_t] * n_pages))


def _decode_kernel(pt_ref, q_ref, kn_ref, vn_ref, bn_ref, bias_ref, g_ref, *rest, npg, t_new):
    k_refs = rest[:npg]
    v_refs = rest[npg:2 * npg]
    o_ref, m_ref, l_ref, acc_ref = rest[2 * npg:]
    c = pl.program_id(1)
    q = q_ref[...]
    rows = q.shape[0]

    @pl.when(c == 0)
    def _():
        s = lax.dot_general(q, kn_ref[...], NT_DIMS, preferred_element_type=F32) + bn_ref[...]
        m = jnp.max(s, axis=1, keepdims=True)
        p = jnp.exp2(s - m)
        m_ref[...] = m
        l_ref[...] = jnp.sum(p, axis=1, keepdims=True)
        acc_ref[...] = jnp.dot(p.astype(BF16), vn_ref[...], preferred_element_type=F32)

    ss = []
    for p in range(npg):
        kt = k_refs[p][...].reshape(D_FOX, PAGE).astype(BF16)
        b8 = bias_ref[:, p * PAGE:(p + 1) * PAGE] * LOG2E
        s = jnp.dot(q, kt, preferred_element_type=F32)
        ss.append(s + jnp.concatenate([b8] * t_new, axis=0))
    m_cur = ss[0]
    for p in range(1, npg):
        m_cur = jnp.maximum(m_cur, ss[p])
    m_prev = m_ref[...]
    m_new = jnp.maximum(m_prev, jnp.max(m_cur, axis=1, keepdims=True))
    alpha = jnp.exp2(m_prev - m_new)
    acc = alpha * acc_ref[...]
    lsum = jnp.zeros((rows, PAGE), F32)
    for p in range(npg):
        pp = jnp.exp2(ss[p] - m_new)
        lsum = lsum + pp
        vt = v_refs[p][...].reshape(D_FOX, PAGE).astype(BF16)
        acc = acc + lax.dot_general(pp.astype(BF16), vt, NT_DIMS, preferred_element_type=F32)
    m_ref[...] = m_new
    l_ref[...] = alpha * l_ref[...] + jnp.sum(lsum, axis=1, keepdims=True)
    acc_ref[...] = acc

    @pl.when(c == pl.num_programs(1) - 1)
    def _():
        o = acc_ref[...] / l_ref[...]
        rowh = lax.broadcasted_iota(jnp.int32, (rows, D_FOX), 0) % H_F
        colh = lax.broadcasted_iota(jnp.int32, (rows, D_FOX), 1) // DH_F
        o = jnp.where(rowh == colh, o, 0.0)
        o4 = jnp.sum(o.reshape(t_new, H_F, D_FOX), axis=1)
        o_ref[...] = o4 * g_ref[...]


def _decode(page_table, qbd, kn, vn, bias_new, bias_past, sfg, cache_kt, cache_vt, *, layer, npg, t_new):
    db, n_pages = page_table.shape
    rows = qbd.shape[1]
    per_b = lambda a: pl.BlockSpec((None,) + a.shape[1:], lambda b, c, pt: (b,) + (0,) * (a.ndim - 1))

    def page_spec(p):
        return pl.BlockSpec((None, None, H_F, DH_F, PAGE),
                            lambda b, c, pt: (layer, pt[b, c * npg + p], 0, 0, 0))

    grid_spec = pltpu.PrefetchScalarGridSpec(
        num_scalar_prefetch=1,
        grid=(db, n_pages // npg),
        in_specs=([per_b(qbd), per_b(kn), per_b(vn), per_b(bias_new),
                   pl.BlockSpec((None, None, H_F, npg * PAGE), lambda b, c, pt: (layer, b, 0, c)),
                   per_b(sfg)]
                  + [page_spec(p) for p in range(npg)] + [page_spec(p) for p in range(npg)]),
        out_specs=pl.BlockSpec((None, t_new, D_FOX), lambda b, c, pt: (b, 0, 0)),
        scratch_shapes=[pltpu.VMEM((rows, 1), F32), pltpu.VMEM((rows, 1), F32),
                        pltpu.VMEM((rows, D_FOX), F32)],
    )
    return pl.pallas_call(
        functools.partial(_decode_kernel, npg=npg, t_new=t_new),
        grid_spec=grid_spec,
        out_shape=jax.ShapeDtypeStruct((db, t_new, D_FOX), F32),
        compiler_params=_cparams(("arbitrary", "arbitrary")),
        name="decode",
    )(page_table, qbd, kn, vn, bias_new, bias_past, sfg, *([cache_kt] * npg), *([cache_vt] * npg))


def _merge_kernel(x_ref, ur_ref, uf_ref, w_ref, y_ref):
    y_ref[...] = (x_ref[...]
                  + jnp.dot(ur_ref[...], w_ref[:D_RET, :], preferred_element_type=F32)
                  + jnp.dot(uf_ref[...], w_ref[D_RET:, :], preferred_element_type=F32))


def _merge(x2d, u_ret, u_fox, w_out, *, tm):
    n, d_model = x2d.shape
    row = lambda i: (i, 0)
    return pl.pallas_call(
        _merge_kernel,
        grid=(n // tm,),
        in_specs=[pl.BlockSpec((tm, d_model), row), pl.BlockSpec((tm, D_RET), row),
                  pl.BlockSpec((tm, D_FOX), row), pl.BlockSpec(w_out.shape, lambda i: (0, 0))],
        out_specs=pl.BlockSpec((tm, d_model), row),
        out_shape=jax.ShapeDtypeStruct((n, d_model), F32),
        compiler_params=_cparams(("parallel",)),
        name="merge",
    )(x2d, u_ret, u_fox, w_out)


def _rope_tables(pos):
    inv = 1.0 / (ROPE_BASE ** (jnp.arange(0, DK_R, 2, dtype=F32) / DK_R))
    ang = pos[:, None] * inv[None, :]
    cos = jnp.cos(ang)
    sin = jnp.sin(ang)
    return jnp.concatenate([cos, cos], axis=-1), jnp.concatenate([-sin, sin], axis=-1)


def kernel(x_prompt, x_sample, cache_fox_k, cache_fox_v, cache_fox_logf, state_ret, page_table,
           norm_gain, w_in, ret_norm_gain, fox_q_gain, fox_k_gain, fox_b_f, w_out):
    batch, seq, d_model = x_prompt.shape
    db, t_new, _ = x_sample.shape
    depth = w_in.shape[0]
    n_pool = cache_fox_k.shape[1]
    n_pages = page_table.shape[1]
    past_len = n_pages * PAGE
    n_p = batch * seq
    n_s = db * t_new
    pad_new = 16
    rows_s = t_new * H_F

    tm_p = min(512, n_p)
    tq = min(2048, seq)
    tk = min(512, seq)
    ret_tile = min(512, seq)
    ret_chunk = min(2 * RET_CHUNK, seq)
    npg = min(32, n_pages)

    w_main = w_in[:, :, :4 * D_RET + 4 * D_FOX].astype(BF16)
    k_col = 4 * D_RET + D_FOX
    wk_t = jnp.swapaxes(w_in[:, :, k_col:k_col + D_FOX], 1, 2).astype(BF16)
    wv_t = jnp.swapaxes(w_in[:, :, k_col + D_FOX:k_col + 2 * D_FOX], 1, 2).astype(BF16)
    wff_t = jnp.swapaxes(w_in[:, :, 4 * D_RET + 4 * D_FOX:], 1, 2).astype(BF16)
    w_out_b = w_out.astype(BF16)
    head_id = jnp.arange(D_FOX, dtype=jnp.int32) // DH_F
    ones_bd = (head_id[:, None] == head_id[None, :]).astype(BF16)
    cos_p, sin_p = _rope_tables(jnp.arange(seq, dtype=F32))
    pos_s = past_len + jnp.arange(t_new, dtype=F32)
    cos_s, sin_s = _rope_tables(jnp.tile(pos_s, db))
    qg = jnp.tile(fox_q_gain, (1, H_F)) * (DH_F ** -0.5 * LOG2E)
    kg_t = jnp.broadcast_to(jnp.tile(fox_k_gain, (1, H_F))[:, :, None], (depth, D_FOX, LANES))
    ret_g = ret_norm_gain.reshape(depth, 1, D_RET)
    kstack_p = jnp.zeros((depth, batch, D_FOX, seq), F32)
    vstack_p = jnp.zeros((depth, batch, D_FOX, seq), F32)

    cache_kt = jnp.transpose(cache_fox_k, (0, 1, 3, 4, 2))
    cache_vt = jnp.transpose(cache_fox_v, (0, 1, 3, 4, 2))
    lf_t = jnp.swapaxes(cache_fox_logf, 2, 3)
    bias_past = _past_bias(lf_t, page_table)

    eye_h = jnp.eye(H_F, dtype=BF16)
    jj = jnp.arange(pad_new)
    new_valid = (jj[None, :] <= jnp.arange(t_new)[:, None]) & (jj[None, :] < t_new)
    s0_p = jnp.zeros((1, batch, H_R, DK_R, DK_R), F32)

    def pad_rows(a):
        a = a.reshape(db, t_new, a.shape[-1])
        return jnp.pad(a, ((0, 0), (0, pad_new - t_new), (0, 0)))

    xp = x_prompt.reshape(n_p, d_model)
    xs = x_sample.reshape(n_s, d_model)
    outs = [[] for _ in range(8)]
    for l in range(depth):
        proj_args = (norm_gain[l][None, :], w_main[l], wk_t[l], wv_t[l], wff_t[l], fox_b_f[l][:, None])
        gains = (qg[l][None, :], kg_t[l], ones_bd)

        (rq, rk, rv, srg, fq, kstack_p, fkt, vstack_p, fvb, sfg, lf, c_t) = _proj(
            xp, *proj_args, cos_p, sin_p, *gains, kstack_p, vstack_p,
            layer=l, tm=tm_p, seg=tm_p, tiles_per_seq=seq // tm_p)
        r3 = lambda a: a.reshape(batch, seq, D_RET)
        u_ret, s_p = _retention(r3(rq), r3(rk), r3(rv), r3(srg), s0_p, ret_g[l], s0_layer=0,
                                tile=ret_tile, chunk=ret_chunk, c_valid=float(ret_chunk), nb=1)
        u_fox = _attention(fq, fkt, fvb, c_t, sfg, batch=batch, seq=seq, tq=tq, tk=tk)
        xp = _merge(xp, u_ret.reshape(n_p, D_RET), u_fox, w_out_b[l], tm=tm_p)
        outs[2].append(lf.T.reshape(batch, seq, H_F))
        outs[3].append(s_p)

        one_slab = jnp.zeros((1, 1, D_FOX, n_s), F32)
        (rq, rk, rv, srg, fq, fkt_s, fktb_s, fvt_s, fvb, sfg, lf, c_t) = _proj(
            xs, *proj_args, cos_s, sin_s, *gains, one_slab, one_slab,
            layer=0, tm=n_s, seg=t_new, tiles_per_seq=1)
        fk = fkt_s[0, 0].T
        fv = fvt_s[0, 0].T
        fkb = fktb_s[0].T
        u_ret, s_s = _retention(pad_rows(rq), pad_rows(rk), pad_rows(rv), pad_rows(srg), state_ret, ret_g[l],
                                s0_layer=l, tile=pad_new, chunk=pad_new, c_valid=float(t_new),
                                nb=math.gcd(db, 4))
        u_ret = u_ret[:, :t_new].reshape(n_s, D_RET)
        q5 = fq.reshape(db, t_new, 1, H_F, DH_F) * eye_h[None, None, :, :, None]
        qbd = q5.reshape(db, rows_s, D_FOX)
        cn = c_t.reshape(H_F, db, t_new).transpose(1, 0, 2)
        bn = jnp.pad(-cn * LOG2E, ((0, 0), (0, 0), (0, pad_new - t_new)))
        bn = jnp.where(new_valid[None, :, None, :], bn[:, None, :, :], NEG)
        bn = bn.reshape(db, rows_s, pad_new)
        u_fox = _decode(page_table, qbd, pad_rows(fkb), pad_rows(fvb), bn, bias_past,
                        sfg.astype(F32).reshape(db, t_new, D_FOX), cache_kt, cache_vt,
                        layer=l, npg=npg, t_new=t_new)
        xs = _merge(xs, u_ret, u_fox.reshape(n_s, D_FOX).astype(BF16), w_out_b[l], tm=n_s)
        outs[4].append(fk.reshape(db, t_new, H_F, DH_F))
        outs[5].append(fv.reshape(db, t_new, H_F, DH_F))
        outs[6].append(lf.T.reshape(db, t_new, H_F))
        outs[7].append(s_s)

    def untranspose(stack):
        return jnp.transpose(stack.reshape(depth, batch, H_F, DH_F, seq), (0, 1, 4, 2, 3))

    return (xp.reshape(batch, seq, d_model), xs.reshape(db, t_new, d_model),
            untranspose(kstack_p), untranspose(vstack_p), jnp.stack(outs[2]), jnp.stack(outs[3]),
            jnp.stack(outs[4]), jnp.stack(outs[5]), jnp.stack(outs[6]), jnp.stack(outs[7]))
```

```python
import functools
import math

import jax
import jax.numpy as jnp
from jax import lax
from jax.experimental import pallas as pl
from jax.experimental.pallas import tpu as pltpu

F32 = jnp.float32
BF16 = jnp.bfloat16

H_R = 4
DK_R = 128
H_F = 8
DH_F = 64
D_RET = H_R * DK_R
D_FOX = H_F * DH_F
PAGE = 128
RET_CHUNK = 128
ROPE_BASE = 10000.0
EPS = 1e-6
NEG = -1e30
LOG2E = math.log2(math.e)

LANES = 128
SUBLANES = 8
VMEM_LIMIT = 56 * 1024 * 1024

NT_DIMS = (((1,), (1,)), ((), ()))
TN_DIMS = (((0,), (0,)), ((), ()))


def _cparams(sem):
    return pltpu.CompilerParams(dimension_semantics=sem, vmem_limit_bytes=VMEM_LIMIT)


def _seg_cumsum(x, seg):
    lane = lax.broadcasted_iota(jnp.int32, x.shape, 1)
    pos = lane & (seg - 1)
    d = 1
    while d < seg:
        shifted = pltpu.roll(x, d, axis=1)
        x = x + jnp.where(pos >= d, shifted, 0.0)
        d *= 2
    return x


def _proj_kernel(x_ref, ng_ref, w_ref, wkt_ref, wvt_ref, wff_ref, bf_ref, cos_ref, sin_ref, qg_ref, kgt_ref,
                 ones_ref, kstack_ref, vstack_ref,
                 rq_ref, rk_ref, rv_ref, srg_ref, fq_ref, fkt_ref, fktb_ref, fvt_ref, fvb_ref, sfg_ref,
                 lf_ref, c_ref, carry_ref, *, seg, tiles_per_seq):
    del kstack_ref, vstack_ref
    i = pl.program_id(0)

    @pl.when(i % tiles_per_seq == 0)
    def _():
        carry_ref[...] = jnp.zeros_like(carry_ref)

    x = x_ref[...]
    ms = jnp.mean(x * x, axis=-1, keepdims=True)
    h = (x * lax.rsqrt(ms + EPS) * ng_ref[...]).astype(BF16)

    def zcol(g):
        return jnp.dot(h, w_ref[:, g * D_RET:(g + 1) * D_RET], preferred_element_type=F32)

    cos = cos_ref[...]
    sin = sin_ref[...]

    def rope_store(z, out_ref, scale):
        for hh in range(H_R):
            zz = z[:, hh * DK_R:(hh + 1) * DK_R]
            rot = pltpu.roll(zz, DK_R // 2, axis=1)
            r = zz * cos + rot * sin
            if scale is not None:
                r = r * scale
            out_ref[:, hh * DK_R:(hh + 1) * DK_R] = r.astype(out_ref.dtype)

    def silu(z):
        return z / (1.0 + jnp.exp(-z))

    def head_norm(z, gain):
        ssq = jnp.dot((z * z).astype(BF16), ones_ref[...], preferred_element_type=F32)
        return z * lax.rsqrt(ssq * (1.0 / DH_F) + EPS) * gain

    rope_store(zcol(0), rq_ref, None)
    rope_store(zcol(1), rk_ref, DK_R ** -0.5)
    rv_ref[...] = zcol(2).astype(BF16)
    srg_ref[...] = silu(zcol(3)).astype(BF16)
    fq_ref[...] = head_norm(zcol(4), qg_ref[...]).astype(BF16)
    zk = lax.dot_general(wkt_ref[...], h, NT_DIMS, preferred_element_type=F32)
    ssq = jnp.dot(ones_ref[...], (zk * zk).astype(BF16), preferred_element_type=F32)
    kgt = jnp.concatenate([kgt_ref[...]] * (zk.shape[1] // LANES), axis=1)
    fkt = zk * lax.rsqrt(ssq * (1.0 / DH_F) + EPS) * kgt
    fkt_ref[...] = fkt
    fktb_ref[...] = fkt.astype(BF16)
    fvt_ref[...] = lax.dot_general(wvt_ref[...], h, NT_DIMS, preferred_element_type=F32)
    fvb_ref[...] = zcol(6).astype(BF16)
    sfg_ref[...] = silu(zcol(7)).astype(BF16)

    ff = lax.dot_general(wff_ref[...], h, NT_DIMS, preferred_element_type=F32) + bf_ref[...]
    lf = jnp.minimum(ff, 0.0) - jnp.log1p(jnp.exp(-jnp.abs(ff)))
    lf_ref[...] = lf
    c_ref[...] = _seg_cumsum(lf, seg) + carry_ref[...]
    carry_ref[...] = carry_ref[...] + jnp.sum(lf, axis=1, keepdims=True)


def _proj(x2d, ng, w_main, wk_t, wv_t, wff_t, bf, cos_t, sin_t, qg, kg_t, ones_bd, kstack, vstack,
          *, layer, tm, seg, tiles_per_seq):
    n, d_model = x2d.shape
    nt = n // tm
    n_seq = kstack.shape[1]
    seq_len = kstack.shape[3]
    row = lambda i: (i, 0)
    const = lambda i: (0, 0)
    tab = lambda i: (i % tiles_per_seq, 0)
    col = lambda i: (0, i)
    bf16_out = jax.ShapeDtypeStruct((n, D_RET), BF16)
    t_out = jax.ShapeDtypeStruct((H_F, n), F32)
    blk = pl.BlockSpec((tm, D_RET), row)
    tblk = pl.BlockSpec((H_F, tm), col)
    stack_blk = pl.BlockSpec((None, None, D_FOX, tm),
                             lambda i: (layer, i // tiles_per_seq, 0, i % tiles_per_seq))
    ktb_blk = pl.BlockSpec((None, D_FOX, tm), lambda i: (i // tiles_per_seq, 0, i % tiles_per_seq))
    stack_out = jax.ShapeDtypeStruct(kstack.shape, F32)
    any_spec = pl.BlockSpec(memory_space=pl.ANY)
    return pl.pallas_call(
        functools.partial(_proj_kernel, seg=seg, tiles_per_seq=tiles_per_seq),
        grid=(nt,),
        in_specs=[
            pl.BlockSpec((tm, d_model), row),
            pl.BlockSpec((1, d_model), const),
            pl.BlockSpec(w_main.shape, const),
            pl.BlockSpec(wk_t.shape, const),
            pl.BlockSpec(wv_t.shape, const),
            pl.BlockSpec(wff_t.shape, const),
            pl.BlockSpec((H_F, 1), const),
            pl.BlockSpec((tm, LANES), tab),
            pl.BlockSpec((tm, LANES), tab),
            pl.BlockSpec((1, D_FOX), const),
            pl.BlockSpec((D_FOX, LANES), const),
            pl.BlockSpec((D_FOX, D_FOX), const),
            any_spec,
            any_spec,
        ],
        out_specs=[blk, blk, blk, blk, blk, stack_blk, ktb_blk, stack_blk, blk, blk, tblk, tblk],
        out_shape=[bf16_out, bf16_out, bf16_out, bf16_out, bf16_out, stack_out,
                   jax.ShapeDtypeStruct((n_seq, D_FOX, seq_len), BF16), stack_out,
                   bf16_out, bf16_out, t_out, t_out],
        input_output_aliases={12: 5, 13: 7},
        scratch_shapes=[pltpu.VMEM((H_F, 1), F32)],
        compiler_params=_cparams(("arbitrary",)),
        name="proj",
    )(x2d, ng, w_main, wk_t, wv_t, wff_t, bf, cos_t, sin_t, qg, kg_t, ones_bd, kstack, vstack)


def _ret_kernel(q_ref, k_ref, v_ref, g_ref, s0_ref, dec_ref, qd_ref, kd_ref, sd_ref, rg_ref,
                u_ref, sfin_ref, state_ref, *, chunk, n_chunks, nb):
    j = pl.program_id(1)

    @pl.when(j == 0)
    def _():
        state_ref[...] = s0_ref[...]

    for bb in range(nb):
        for hh in range(H_R):
            cols = slice(hh * DK_R, (hh + 1) * DK_R)
            s = state_ref[bb, hh]
            for ci in range(n_chunks):
                rows = slice(ci * chunk, (ci + 1) * chunk)
                q = q_ref[bb, rows, cols]
                k = k_ref[bb, rows, cols]
                v = v_ref[bb, rows, cols]
                sc = lax.dot_general(q, k, NT_DIMS, preferred_element_type=F32) * dec_ref[hh]
                o = (jnp.dot(sc.astype(BF16), v, preferred_element_type=F32)
                     + jnp.dot(q, s.astype(BF16), preferred_element_type=F32) * qd_ref[hh])
                kdec = (k.astype(F32) * kd_ref[hh]).astype(BF16)
                s = sd_ref[hh] * s + lax.dot_general(kdec, v, TN_DIMS, preferred_element_type=F32)
                ms = jnp.mean(o * o, axis=-1, keepdims=True)
                u = o * lax.rsqrt(ms + EPS) * rg_ref[:, cols] * g_ref[bb, rows, cols].astype(F32)
                u_ref[bb, rows, cols] = u.astype(BF16)
            state_ref[bb, hh] = s

    @pl.when(j == pl.num_programs(1) - 1)
    def _():
        sfin_ref[...] = state_ref[...]


def _retention_tables(chunk, c_valid):
    lg = jnp.log(1.0 - 2.0 ** (-5.0 - jnp.arange(H_R, dtype=F32)))
    idx = jnp.arange(chunk, dtype=F32)
    diff = idx[:, None] - idx[None, :]
    dec = jnp.where(diff >= 0, jnp.exp(jnp.maximum(diff, 0.0)[None] * lg[:, None, None]), 0.0)
    qd = jnp.exp((idx + 1.0)[None, :] * lg[:, None])
    kd = jnp.exp((c_valid - 1.0 - idx)[None, :] * lg[:, None])
    sd = jnp.exp(c_valid * lg)
    qd = jnp.broadcast_to(qd[:, :, None], (H_R, chunk, DK_R))
    kd = jnp.broadcast_to(kd[:, :, None], (H_R, chunk, DK_R))
    sd = jnp.broadcast_to(sd[:, None, None], (H_R, 1, DK_R))
    return dec, qd, kd, sd


def _retention(q, k, v, g, s0, ret_g, *, s0_layer, tile, chunk, c_valid, nb):
    b, t, _ = q.shape
    dec, qd, kd, sd = _retention_tables(chunk, c_valid)
    seq = pl.BlockSpec((nb, tile, D_RET), lambda bi, j: (bi, j, 0))
    st = pl.BlockSpec((nb, H_R, DK_R, DK_R), lambda bi, j: (bi, 0, 0, 0))
    st_in = pl.BlockSpec((None, nb, H_R, DK_R, DK_R), lambda bi, j: (s0_layer, bi, 0, 0, 0))

    def full(a):
        return pl.BlockSpec(a.shape, lambda bi, j: (0,) * a.ndim)

    return pl.pallas_call(
        functools.partial(_ret_kernel, chunk=chunk, n_chunks=tile // chunk, nb=nb),
        grid=(b // nb, t // tile),
        in_specs=[seq, seq, seq, seq, st_in, full(dec), full(qd), full(kd), full(sd), full(ret_g)],
        out_specs=[seq, st],
        out_shape=[jax.ShapeDtypeStruct((b, t, D_RET), BF16),
                   jax.ShapeDtypeStruct((b, H_R, DK_R, DK_R), F32)],
        scratch_shapes=[pltpu.VMEM((nb, H_R, DK_R, DK_R), F32)],
        compiler_params=_cparams(("arbitrary", "arbitrary")),
        name="retention",
    )(q, k, v, g, s0, dec, qd, kd, sd, ret_g)


def _attn_kernel(q_ref, kt_ref, v_ref, c_ref, g_ref, o_ref, m_ref, acc_ref, *, tq, tk, seq):
    pair = pl.program_id(1)
    nchunk = tk // LANES
    first_q = lax.broadcasted_iota(jnp.int32, (tq, LANES), 1) < DH_F
    first_k = lax.broadcasted_iota(jnp.int32, (tk, LANES), 1) < DH_F

    def q_block(i, carry):
        q0 = pl.multiple_of(i * tq, tq)
        q = q_ref[pl.ds(q0, tq), :]
        zero = jnp.zeros_like(q)
        qs = (jnp.where(first_q, q, zero), jnp.where(first_q, zero, q))
        m_ref[...] = jnp.full(m_ref.shape, NEG, F32)
        acc_ref[...] = jnp.zeros_like(acc_ref)

        def kv_block(j, row0, masked):
            k0 = pl.multiple_of(j * tk, tk)
            kt = kt_ref[:, pl.ds(k0, tk)]
            vb = v_ref[pl.ds(k0, tk), :]
            one = jnp.ones_like(vb)
            vs = (jnp.where(first_k, vb, one), jnp.where(first_k, one, vb))
            for hh in range(2):
                ck = c_ref[pl.ds(2 * pair + hh, 1), pl.ds(k0, tk)] * LOG2E
                s = jnp.dot(qs[hh][row0:], kt, preferred_element_type=F32)
                chunks = []
                for c in range(nchunk):
                    ch = s[:, c * LANES:(c + 1) * LANES] - ck[:, c * LANES:(c + 1) * LANES]
                    if masked:
                        rowi = lax.broadcasted_iota(jnp.int32, ch.shape, 0)
                        coli = lax.broadcasted_iota(jnp.int32, ch.shape, 1)
                        ch = jnp.where(rowi >= coli + c * LANES, ch, NEG)
                    chunks.append(ch)
                mx = chunks[0]
                for c in range(1, nchunk):
                    mx = jnp.maximum(mx, chunks[c])
                m_prev = m_ref[hh, row0:, :]
                m_new = jnp.maximum(m_prev, jnp.max(mx, axis=1, keepdims=True))
                alpha = jnp.exp2(m_prev - m_new)
                p = jnp.concatenate([jnp.exp2(ch - m_new) for ch in chunks], axis=1).astype(BF16)
                acc_ref[hh, row0:, :] = (acc_ref[hh, row0:, :] * alpha
                                         + jnp.dot(p, vs[hh], preferred_element_type=F32))
                m_ref[hh, row0:, :] = m_new

        r = tq // tk
        n_full = i * r

        def body(j, c2):
            kv_block(j, 0, False)
            return c2

        lax.fori_loop(0, n_full, body, 0)
        for jj in range(r):
            kv_block(n_full + jj, jj * tk, True)

        acc_a = acc_ref[0]
        acc_b = acc_ref[1]
        num = jnp.where(first_q, acc_a, acc_b)
        den = jnp.where(first_q, pltpu.roll(acc_a, DH_F, axis=1), pltpu.roll(acc_b, DH_F, axis=1))
        o_ref[pl.ds(q0, tq), :] = (num / den * g_ref[pl.ds(q0, tq), :].astype(F32)).astype(BF16)
        return carry

    lax.fori_loop(0, seq // tq, q_block, 0)


def _attention(fq, fkt, fvb, c_t, sfg, *, batch, seq, tq, tk):
    n = batch * seq
    blk = pl.BlockSpec((seq, LANES), lambda b, p: (b, p))
    ktblk = pl.BlockSpec((None, LANES, seq), lambda b, p: (b, p, 0))
    cblk = pl.BlockSpec((H_F, seq), lambda b, p: (0, b))
    return pl.pallas_call(
        functools.partial(_attn_kernel, tq=tq, tk=tk, seq=seq),
        grid=(batch, H_F // 2),
        in_specs=[blk, ktblk, blk, cblk, blk],
        out_specs=blk,
        out_shape=jax.ShapeDtypeStruct((n, D_FOX), BF16),
        scratch_shapes=[pltpu.VMEM((2, tq, LANES), F32), pltpu.VMEM((2, tq, LANES), F32)],
        compiler_params=_cparams(("arbitrary", "arbitrary")),
        name="attention",
    )(fq, fkt, fvb, c_t, sfg)


def _bias_kernel(pt_ref, tri_ref, *rest, n_pages):
    del pt_ref
    page_refs = rest[:n_pages]
    out_ref = rest[n_pages]
    x3 = jnp.stack([r[...] for r in page_refs], axis=0)

    x = x3.reshape(n_pages * H_F, PAGE)
    x_hi = x.astype(BF16)
    r1 = x - x_hi.astype(F32)
    x_mid = r1.astype(BF16)
    x_lo = (r1 - x_mid.astype(F32)).astype(BF16)
    tri = tri_ref[...]
    s = (jnp.dot(x_hi, tri, preferred_element_type=F32) + jnp.dot(x_mid, tri, preferred_element_type=F32)
         + jnp.dot(x_lo, tri, preferred_element_type=F32))
    s3 = s.reshape(n_pages, H_F, PAGE)
    tot = jnp.broadcast_to(s3[:, :, 0:1], s3.shape)
    e = tot
    d = 1
    while d < n_pages:
        e = e + jnp.concatenate([e[d:], jnp.zeros((d,) + e.shape[1:], F32)], axis=0)
        d *= 2
    res = s3 - x3 + (e - tot)
    for p in range(n_pages):
        out_ref[:, p * PAGE:(p + 1) * PAGE] = res[p]


def _past_bias(lf_t, page_table):
    depth = lf_t.shape[0]
    db, n_pages = page_table.shape

    def page_index(p, li, b, pt):
        return (li, pt[b, p], 0, 0)

    grid_spec = pltpu.PrefetchScalarGridSpec(
        num_scalar_prefetch=1,
        grid=(depth, db),
        in_specs=([pl.BlockSpec((PAGE, PAGE), lambda li, b, pt: (0, 0))]
                  + [pl.BlockSpec((None, None, H_F, PAGE), functools.partial(page_index, p))
                     for p in range(n_pages)]),
        out_specs=pl.BlockSpec((None, None, H_F, n_pages * PAGE), lambda li, b, pt: (li, b, 0, 0)),
    )
    idx = jnp.arange(PAGE)
    tri = (idx[:, None] >= idx[None, :]).astype(BF16)
    return pl.pallas_call(
        functools.partial(_bias_kernel, n_pages=n_pages),
        grid_spec=grid_spec,
        out_shape=jax.ShapeDtypeStruct((depth, db, H_F, n_pages * PAGE), F32),
        compiler_params=_cparams(("arbitrary", "arbitrary")),
        name="past_bias",
    )(page_table, tri, *([lf_t] * n_pages))


def _decode_kernel(pt_ref, q_ref, kn_ref, vn_ref, bn_ref, tri_ref, g_ref, *rest, npg, t_new):
    k_refs = rest[:npg]
    v_refs = rest[npg:2 * npg]
    lf_refs = rest[2 * npg:3 * npg]
    o_ref, m_ref, l_ref, acc_ref, carry_ref = rest[3 * npg:]
    c = pl.program_id(1)
    q = q_ref[...]
    rows = q.shape[0]

    @pl.when(c == 0)
    def _():
        s = lax.dot_general(q, kn_ref[...], NT_DIMS, preferred_element_type=F32) + bn_ref[...]
        m = jnp.max(s, axis=1, keepdims=True)
        p = jnp.exp2(s - m)
        m_ref[...] = m
        l_ref[...] = jnp.sum(p, axis=1, keepdims=True)
        acc_ref[...] = jnp.dot(p.astype(BF16), vn_ref[...], preferred_element_type=F32)
        carry_ref[...] = jnp.zeros_like(carry_ref)

    x3 = jnp.stack([r[...] for r in lf_refs], axis=0)
    x = x3.reshape(npg * H_F, PAGE)
    x_hi = x.astype(BF16)
    r1 = x - x_hi.astype(F32)
    x_mid = r1.astype(BF16)
    x_lo = (r1 - x_mid.astype(F32)).astype(BF16)
    tri = tri_ref[...]
    s3 = (jnp.dot(x_hi, tri, preferred_element_type=F32) + jnp.dot(x_mid, tri, preferred_element_type=F32)
          + jnp.dot(x_lo, tri, preferred_element_type=F32)).reshape(npg, H_F, PAGE)
    tot = jnp.broadcast_to(s3[:, :, 0:1], s3.shape)
    e = tot
    d = 1
    while d < npg:
        e = e + jnp.concatenate([e[d:], jnp.zeros((d,) + e.shape[1:], F32)], axis=0)
        d *= 2
    carry = carry_ref[...]
    later = (s3 - x3 + (e - tot) + carry[None]) * LOG2E
    carry_ref[...] = carry + e[0]

    ss = []
    for p in range(npg):
        kt = k_refs[p][...].reshape(D_FOX, PAGE).astype(BF16)
        b8 = later[p]
        s = jnp.dot(q, kt, preferred_element_type=F32)
        ss.append(s + jnp.concatenate([b8] * t_new, axis=0))
    m_cur = ss[0]
    for p in range(1, npg):
        m_cur = jnp.maximum(m_cur, ss[p])
    m_prev = m_ref[...]
    m_new = jnp.maximum(m_prev, jnp.max(m_cur, axis=1, keepdims=True))
    alpha = jnp.exp2(m_prev - m_new)
    acc = alpha * acc_ref[...]
    lsum = jnp.zeros((rows, PAGE), F32)
    for p in range(npg):
        pp = jnp.exp2(ss[p] - m_new)
        lsum = lsum + pp
        vt = v_refs[p][...].reshape(D_FOX, PAGE).astype(BF16)
        acc = acc + lax.dot_general(pp.astype(BF16), vt, NT_DIMS, preferred_element_type=F32)
    m_ref[...] = m_new
    l_ref[...] = alpha * l_ref[...] + jnp.sum(lsum, axis=1, keepdims=True)
    acc_ref[...] = acc

    @pl.when(c == pl.num_programs(1) - 1)
    def _():
        o = acc_ref[...] / l_ref[...]
        rowh = lax.broadcasted_iota(jnp.int32, (rows, D_FOX), 0) % H_F
        colh = lax.broadcasted_iota(jnp.int32, (rows, D_FOX), 1) // DH_F
        o = jnp.where(rowh == colh, o, 0.0)
        o4 = jnp.sum(o.reshape(t_new, H_F, D_FOX), axis=1)
        o_ref[...] = o4 * g_ref[...]


def _decode(page_table, qbd, kn, vn, bias_new, sfg, cache_kt, cache_vt, lf_t, *, layer, npg, t_new):
    db, n_pages = page_table.shape
    rows = qbd.shape[1]
    n_chunks = n_pages // npg
    per_b = lambda a: pl.BlockSpec((None,) + a.shape[1:], lambda b, c, pt: (b,) + (0,) * (a.ndim - 1))

    def page_of(p, b, c, pt):
        return pt[b, (n_chunks - 1 - c) * npg + p]

    def page_spec(p):
        return pl.BlockSpec((None, None, H_F, DH_F, PAGE),
                            lambda b, c, pt: (layer, page_of(p, b, c, pt), 0, 0, 0))

    def lf_spec(p):
        return pl.BlockSpec((None, None, H_F, PAGE), lambda b, c, pt: (layer, page_of(p, b, c, pt), 0, 0))

    idx = jnp.arange(PAGE)
    tri = (idx[:, None] >= idx[None, :]).astype(BF16)
    grid_spec = pltpu.PrefetchScalarGridSpec(
        num_scalar_prefetch=1,
        grid=(db, n_chunks),
        in_specs=([per_b(qbd), per_b(kn), per_b(vn), per_b(bias_new),
                   pl.BlockSpec((PAGE, PAGE), lambda b, c, pt: (0, 0)),
                   per_b(sfg)]
                  + [page_spec(p) for p in range(npg)] + [page_spec(p) for p in range(npg)]
                  + [lf_spec(p) for p in range(npg)]),
        out_specs=pl.BlockSpec((None, t_new, D_FOX), lambda b, c, pt: (b, 0, 0)),
        scratch_shapes=[pltpu.VMEM((rows, 1), F32), pltpu.VMEM((rows, 1), F32),
                        pltpu.VMEM((rows, D_FOX), F32), pltpu.VMEM((H_F, PAGE), F32)],
    )
    return pl.pallas_call(
        functools.partial(_decode_kernel, npg=npg, t_new=t_new),
        grid_spec=grid_spec,
        out_shape=jax.ShapeDtypeStruct((db, t_new, D_FOX), F32),
        compiler_params=_cparams(("arbitrary", "arbitrary")),
        name="decode",
    )(page_table, qbd, kn, vn, bias_new, tri, sfg, *([cache_kt] * npg), *([cache_vt] * npg),
      *([lf_t] * npg))


def _merge_kernel(x_ref, ur_ref, uf_ref, w_ref, y_ref):
    y_ref[...] = (x_ref[...]
                  + jnp.dot(ur_ref[...], w_ref[:D_RET, :], preferred_element_type=F32)
                  + jnp.dot(uf_ref[...], w_ref[D_RET:, :], preferred_element_type=F32))


def _merge(x2d, u_ret, u_fox, w_out, *, tm):
    n, d_model = x2d.shape
    row = lambda i: (i, 0)
    return pl.pallas_call(
        _merge_kernel,
        grid=(n // tm,),
        in_specs=[pl.BlockSpec((tm, d_model), row), pl.BlockSpec((tm, D_RET), row),
                  pl.BlockSpec((tm, D_FOX), row), pl.BlockSpec(w_out.shape, lambda i: (0, 0))],
        out_specs=pl.BlockSpec((tm, d_model), row),
        out_shape=jax.ShapeDtypeStruct((n, d_model), F32),
        compiler_params=_cparams(("parallel",)),
        name="merge",
    )(x2d, u_ret, u_fox, w_out)


def _rope_tables(pos):
    inv = 1.0 / (ROPE_BASE ** (jnp.arange(0, DK_R, 2, dtype=F32) / DK_R))
    ang = pos[:, None] * inv[None, :]
    cos = jnp.cos(ang)
    sin = jnp.sin(ang)
    return jnp.concatenate([cos, cos], axis=-1), jnp.concatenate([-sin, sin], axis=-1)


def kernel(x_prompt, x_sample, cache_fox_k, cache_fox_v, cache_fox_logf, state_ret, page_table,
           norm_gain, w_in, ret_norm_gain, fox_q_gain, fox_k_gain, fox_b_f, w_out):
    batch, seq, d_model = x_prompt.shape
    db, t_new, _ = x_sample.shape
    depth = w_in.shape[0]
    n_pool = cache_fox_k.shape[1]
    n_pages = page_table.shape[1]
    past_len = n_pages * PAGE
    n_p = batch * seq
    n_s = db * t_new
    pad_new = 16
    rows_s = t_new * H_F

    tm_p = min(512, n_p)
    tq = min(2048, seq)
    tk = min(512, seq)
    ret_tile = min(512, seq)
    ret_chunk = min(2 * RET_CHUNK, seq)
    npg = min(32, n_pages)

    w_main = w_in[:, :, :4 * D_RET + 4 * D_FOX].astype(BF16)
    k_col = 4 * D_RET + D_FOX
    wk_t = jnp.swapaxes(w_in[:, :, k_col:k_col + D_FOX], 1, 2).astype(BF16)
    wv_t = jnp.swapaxes(w_in[:, :, k_col + D_FOX:k_col + 2 * D_FOX], 1, 2).astype(BF16)
    wff_t = jnp.swapaxes(w_in[:, :, 4 * D_RET + 4 * D_FOX:], 1, 2).astype(BF16)
    w_out_b = w_out.astype(BF16)
    head_id = jnp.arange(D_FOX, dtype=jnp.int32) // DH_F
    ones_bd = (head_id[:, None] == head_id[None, :]).astype(BF16)
    cos_p, sin_p = _rope_tables(jnp.arange(seq, dtype=F32))
    pos_s = past_len + jnp.arange(t_new, dtype=F32)
    cos_s, sin_s = _rope_tables(jnp.tile(pos_s, db))
    qg = jnp.tile(fox_q_gain, (1, H_F)) * (DH_F ** -0.5 * LOG2E)
    kg_t = jnp.broadcast_to(jnp.tile(fox_k_gain, (1, H_F))[:, :, None], (depth, D_FOX, LANES))
    ret_g = ret_norm_gain.reshape(depth, 1, D_RET)
    kstack_p = jnp.zeros((depth, batch, D_FOX, seq), F32)
    vstack_p = jnp.zeros((depth, batch, D_FOX, seq), F32)

    cache_kt = jnp.transpose(cache_fox_k, (0, 1, 3, 4, 2))
    cache_vt = jnp.transpose(cache_fox_v, (0, 1, 3, 4, 2))
    lf_t = jnp.swapaxes(cache_fox_logf, 2, 3)

    eye_h = jnp.eye(H_F, dtype=BF16)
    jj = jnp.arange(pad_new)
    new_valid = (jj[None, :] <= jnp.arange(t_new)[:, None]) & (jj[None, :] < t_new)
    s0_p = jnp.zeros((1, batch, H_R, DK_R, DK_R), F32)

    def pad_rows(a):
        a = a.reshape(db, t_new, a.shape[-1])
        return jnp.pad(a, ((0, 0), (0, pad_new - t_new), (0, 0)))

    xp = x_prompt.reshape(n_p, d_model)
    xs = x_sample.reshape(n_s, d_model)
    outs = [[] for _ in range(8)]
    for l in range(depth):
        proj_args = (norm_gain[l][None, :], w_main[l], wk_t[l], wv_t[l], wff_t[l], fox_b_f[l][:, None])
        gains = (qg[l][None, :], kg_t[l], ones_bd)

        (rq, rk, rv, srg, fq, kstack_p, fkt, vstack_p, fvb, sfg, lf, c_t) = _proj(
            xp, *proj_args, cos_p, sin_p, *gains, kstack_p, vstack_p,
            layer=l, tm=tm_p, seg=tm_p, tiles_per_seq=seq // tm_p)
        r3 = lambda a: a.reshape(batch, seq, D_RET)
        u_ret, s_p = _retention(r3(rq), r3(rk), r3(rv), r3(srg), s0_p, ret_g[l], s0_layer=0,
                                tile=ret_tile, chunk=ret_chunk, c_valid=float(ret_chunk), nb=1)
        u_fox = _attention(fq, fkt, fvb, c_t, sfg, batch=batch, seq=seq, tq=tq, tk=tk)
        xp = _merge(xp, u_ret.reshape(n_p, D_RET), u_fox, w_out_b[l], tm=tm_p)
        outs[2].append(lf.T.reshape(batch, seq, H_F))
        outs[3].append(s_p)

        one_slab = jnp.zeros((1, 1, D_FOX, n_s), F32)
        (rq, rk, rv, srg, fq, fkt_s, fktb_s, fvt_s, fvb, sfg, lf, c_t) = _proj(
            xs, *proj_args, cos_s, sin_s, *gains, one_slab, one_slab,
            layer=0, tm=n_s, seg=t_new, tiles_per_seq=1)
        fk = fkt_s[0, 0].T
        fv = fvt_s[0, 0].T
        fkb = fktb_s[0].T
        u_ret, s_s = _retention(pad_rows(rq), pad_rows(rk), pad_rows(rv), pad_rows(srg), state_ret, ret_g[l],
                                s0_layer=l, tile=pad_new, chunk=pad_new, c_valid=float(t_new),
                                nb=math.gcd(db, 4))
        u_ret = u_ret[:, :t_new].reshape(n_s, D_RET)
        q5 = fq.reshape(db, t_new, 1, H_F, DH_F) * eye_h[None, None, :, :, None]
        qbd = q5.reshape(db, rows_s, D_FOX)
        cn = c_t.reshape(H_F, db, t_new).transpose(1, 0, 2)
        bn = jnp.pad(-cn * LOG2E, ((0, 0), (0, 0), (0, pad_new - t_new)))
        bn = jnp.where(new_valid[None, :, None, :], bn[:, None, :, :], NEG)
        bn = bn.reshape(db, rows_s, pad_new)
        u_fox = _decode(page_table, qbd, pad_rows(fkb), pad_rows(fvb), bn,
                        sfg.astype(F32).reshape(db, t_new, D_FOX), cache_kt, cache_vt, lf_t,
                        layer=l, npg=npg, t_new=t_new)
        xs = _merge(xs, u_ret, u_fox.reshape(n_s, D_FOX).astype(BF16), w_out_b[l], tm=n_s)
        outs[4].append(fk.reshape(db, t_new, H_F, DH_F))
        outs[5].append(fv.reshape(db, t_new, H_F, DH_F))
        outs[6].append(lf.T.reshape(db, t_new, H_F))
        outs[7].append(s_s)

    def untranspose(stack):
        return jnp.transpose(stack.reshape(depth, batch, H_F, DH_F, seq), (0, 1, 4, 2, 3))

    return (xp.reshape(batch, seq, d_model), xs.reshape(db, t_new, d_model),
            untranspose(kstack_p), untranspose(vstack_p), jnp.stack(outs[2]), jnp.stack(outs[3]),
            jnp.stack(outs[4]), jnp.stack(outs[5]), jnp.stack(outs[6]), jnp.stack(outs[7]))
```
